```python
import jax, jax.numpy as jnp
from jax import lax
import numpy as np

D_MODEL = 1024
BATCH = 8
SEQ = 2048
DEPTH = 4

CHUNK = 64
HEAD_DIM = 64
RWKV_WIDTH = D_MODEL // 2
RWKV_HEADS = RWKV_WIDTH // HEAD_DIM
SC_WIDTH = D_MODEL // 4
SC_KERNEL = 3
CF_WIDTH = D_MODEL // 4
CF_KERNEL = 31
MIX_WIDTH = RWKV_WIDTH + SC_WIDTH + CF_WIDTH
IN_WIDTH = 3 * RWKV_WIDTH + 3 * SC_WIDTH + 2 * CF_WIDTH
DECAY_RANK = 64
ICLR_RANK = 64
GATE_RANK = 128
FFN_HIDDEN = ((8 * D_MODEL + 3 * 256 - 1) // (3 * 256)) * 256
NORM_EPS = 1e-6
LN_EPS = 1e-5
GN_EPS = 64e-5

kernel_name = "hymba_rwkv7_shortconv_conformer_trunk"


def rms_norm(x, g):
    x32 = x.astype(jnp.float32)
    y = x32 * lax.rsqrt(jnp.mean(x32 * x32, axis=-1, keepdims=True) + NORM_EPS)
    return (y * g.astype(jnp.float32)).astype(x.dtype)


def layer_norm(x, g, b):
    x32 = x.astype(jnp.float32)
    mu = jnp.mean(x32, axis=-1, keepdims=True)
    var = jnp.mean(jnp.square(x32 - mu), axis=-1, keepdims=True)
    y = (x32 - mu) * lax.rsqrt(var + LN_EPS)
    return (y * g.astype(jnp.float32) + b.astype(jnp.float32)).astype(x.dtype)


def token_shift(u):
    return jnp.pad(u[:, :-1], ((0, 0), (1, 0), (0, 0)))


def causal_dwconv(u, w):
    k_width, ch = w.shape
    return lax.conv_general_dilated(
        u, w[:, None, :].astype(u.dtype), window_strides=(1,), padding=[(k_width - 1, 0)],
        dimension_numbers=("NWC", "WIO", "NWC"), feature_group_count=ch)


def rwkv7_scan(r, w, k, v, a_vec, b_vec):
    xs = tuple(jnp.moveaxis(t, 1, 0) for t in (r, w, k, v, a_vec, b_vec))

    def step(S, inp):
        r_t, w_t, k_t, v_t, a_t, b_t = inp
        sa = jnp.einsum("bhij,bhj->bhi", S, a_t)
        S = S * w_t[:, :, None, :] + sa[..., None] * b_t[:, :, None, :] + v_t[..., None] * k_t[:, :, None, :]
        y = jnp.einsum("bhij,bhj->bhi", S, r_t)
        return S, y

    bsz, _, nh, nd = r.shape
    s0 = jnp.zeros((bsz, nh, nd, nd), jnp.float32)
    _, ys = lax.scan(step, s0, xs)
    return jnp.moveaxis(ys, 0, 1)


def rwkv7_group(h, rkv, mu_rkv, mu_wag, w0, w1, w2, a0, a1, a2, g1, g2, k_k, k_a, r_k, lnx_g, lnx_b):
    bsz, seq, _ = h.shape
    rkv = rkv + (token_shift(rkv) - rkv) * mu_rkv
    r, k, v = jnp.split(rkv, 3, axis=-1)
    dh = token_shift(h) - h
    xw = h + dh * mu_wag[0]
    xa = h + dh * mu_wag[1]
    xg = h + dh * mu_wag[2]
    w_log = -jax.nn.softplus(-(w0 + jnp.tanh(xw @ w1) @ w2).astype(jnp.float32)) - 0.5
    decay = jnp.exp(-jnp.exp(w_log))
    a = jax.nn.sigmoid((a0 + (xa @ a1) @ a2).astype(jnp.float32))
    g = jax.nn.sigmoid(xg @ g1) @ g2
    hs = lambda t: t.reshape(bsz, seq, RWKV_HEADS, HEAD_DIM)
    hp = lambda p: p.reshape(RWKV_HEADS, HEAD_DIM).astype(jnp.float32)
    r32, k32, v32 = (hs(t).astype(jnp.float32) for t in (r, k, v))
    a_h, w_h = hs(a), hs(decay)
    kk = k32 * hp(k_k)
    kk = kk / jnp.maximum(jnp.sqrt(jnp.sum(kk * kk, axis=-1, keepdims=True)), 1e-12)
    k32 = k32 * (1.0 + (a_h - 1.0) * hp(k_a))
    y = rwkv7_scan(r32, w_h, k32, v32, -kk, kk * a_h)
    mu = jnp.mean(y, axis=-1, keepdims=True)
    var = jnp.mean(jnp.square(y - mu), axis=-1, keepdims=True)
    y = (y - mu) * lax.rsqrt(var + GN_EPS)
    y = y * hp(lnx_g) + hp(lnx_b)
    bonus = jnp.sum(r32 * k32 * r_k.astype(jnp.float32), axis=-1, keepdims=True) * v32
    y = (y + bonus).reshape(bsz, seq, RWKV_WIDTH).astype(h.dtype)
    return y * g


def shortconv_group(sc_in, conv_w):
    gate_b, gate_c, u = jnp.split(sc_in, 3, axis=-1)
    return gate_b * causal_dwconv(gate_c * u, conv_w)


def conformer_group(cf_in, conv_w, conv_b, ln_g, ln_b):
    val, gate = jnp.split(cf_in, 2, axis=-1)
    u = val * jax.nn.sigmoid(gate)
    u = causal_dwconv(u, conv_w) + conv_b
    return jax.nn.silu(layer_norm(u, ln_g, ln_b))


def setup_inputs(seed: int = 0) -> dict:
    key = jax.random.key(seed)
    ks = iter(jax.random.split(key, 40))
    nrm = lambda shape, s: jax.random.normal(next(ks), shape, jnp.float32) * s
    unif = lambda shape: jax.random.uniform(next(ks), shape, jnp.float32)
    L = DEPTH
    return {
        "x": nrm((BATCH, SEQ, D_MODEL), 1.0),
        "w_in": nrm((L, D_MODEL, IN_WIDTH), D_MODEL ** -0.5),
        "mu_rkv": unif((L, 3 * RWKV_WIDTH)),
        "mu_wag": unif((L, 3, D_MODEL)),
        "w0": nrm((L, RWKV_WIDTH), 0.5),
        "w1": nrm((L, D_MODEL, DECAY_RANK), D_MODEL ** -0.5),
        "w2": nrm((L, DECAY_RANK, RWKV_WIDTH), 0.5 * DECAY_RANK ** -0.5),
        "a0": nrm((L, RWKV_WIDTH), 0.5),
        "a1": nrm((L, D_MODEL, ICLR_RANK), D_MODEL ** -0.5),
        "a2": nrm((L, ICLR_RANK, RWKV_WIDTH), 0.5 * ICLR_RANK ** -0.5),
        "g1": nrm((L, D_MODEL, GATE_RANK), D_MODEL ** -0.5),
        "g2": nrm((L, GATE_RANK, RWKV_WIDTH), GATE_RANK ** -0.5),
        "k_k": 0.85 + nrm((L, RWKV_WIDTH), 0.05),
        "k_a": 1.0 + nrm((L, RWKV_WIDTH), 0.05),
        "r_k": nrm((L, RWKV_HEADS, HEAD_DIM), 0.1),
        "lnx_g": 1.0 + nrm((L, RWKV_WIDTH), 0.05),
        "lnx_b": nrm((L, RWKV_WIDTH), 0.01),
        "sc_conv_w": nrm((L, SC_KERNEL, SC_WIDTH), SC_KERNEL ** -0.5),
        "cf_conv_w": nrm((L, CF_KERNEL, CF_WIDTH), CF_KERNEL ** -0.5),
        "cf_conv_b": nrm((L, CF_WIDTH), 0.01),
        "cf_ln_g": 1.0 + nrm((L, CF_WIDTH), 0.05),
        "cf_ln_b": nrm((L, CF_WIDTH), 0.01),
        "w_o": nrm((L, MIX_WIDTH, D_MODEL), MIX_WIDTH ** -0.5),
        "w_gate": nrm((L, D_MODEL, FFN_HIDDEN), D_MODEL ** -0.5),
        "w_up": nrm((L, D_MODEL, FFN_HIDDEN), D_MODEL ** -0.5),
        "w_down": nrm((L, FFN_HIDDEN, D_MODEL), FFN_HIDDEN ** -0.5),
        "pre_mix_g": 1.0 + nrm((L, D_MODEL), 0.05),
        "post_mix_g": 1.0 + nrm((L, D_MODEL), 0.05),
        "pre_ffn_g": 1.0 + nrm((L, D_MODEL), 0.05),
        "post_ffn_g": 1.0 + nrm((L, D_MODEL), 0.05),
    }


def reference(x, w_in, mu_rkv, mu_wag, w0, w1, w2, a0, a1, a2, g1, g2, k_k, k_a, r_k,
              lnx_g, lnx_b, sc_conv_w, cf_conv_w, cf_conv_b, cf_ln_g, cf_ln_b, w_o,
              w_gate, w_up, w_down, pre_mix_g, post_mix_g, pre_ffn_g, post_ffn_g):
    split_at = (3 * RWKV_WIDTH, 3 * RWKV_WIDTH + 3 * SC_WIDTH)
    for l in range(DEPTH):
        h = rms_norm(x, pre_mix_g[l])
        proj = h @ w_in[l]
        rkv, sc_in, cf_in = jnp.split(proj, split_at, axis=-1)
        y_rwkv = rwkv7_group(h, rkv, mu_rkv[l], mu_wag[l], w0[l], w1[l], w2[l], a0[l], a1[l], a2[l],
                             g1[l], g2[l], k_k[l], k_a[l], r_k[l], lnx_g[l], lnx_b[l])
        y_sc = shortconv_group(sc_in, sc_conv_w[l])
        y_cf = conformer_group(cf_in, cf_conv_w[l], cf_conv_b[l], cf_ln_g[l], cf_ln_b[l])
        mix = jnp.concatenate([y_rwkv, y_sc, y_cf], axis=-1) @ w_o[l]
        x = x + rms_norm(mix, post_mix_g[l])
        h2 = rms_norm(x, pre_ffn_g[l])
        f = (jax.nn.silu(h2 @ w_gate[l]) * (h2 @ w_up[l])) @ w_down[l]
        x = x + rms_norm(f, post_ffn_g[l])
    return x
```

```python
import functools

import jax
import jax.numpy as jnp
from jax import lax
from jax.experimental import pallas as pl
from jax.experimental.pallas import tpu as pltpu

D_MODEL = 1024
HEAD_DIM = 64
RWKV_WIDTH = 512
SC_WIDTH = 256
CF_WIDTH = 256
SC_KERNEL = 3
CF_KERNEL = 31
DECAY_RANK = 64
ICLR_RANK = 64
GATE_RANK = 128
LOW_WIDTH = DECAY_RANK + ICLR_RANK + GATE_RANK
IN_WIDTH = 3 * RWKV_WIDTH + 3 * SC_WIDTH + 2 * CF_WIDTH
PROJ_WIDTH = IN_WIDTH + 2 * LOW_WIDTH
NORM_EPS = 1e-6
LN_EPS = 1e-5
GN_EPS = 64e-5

CHUNK = 64
PAIR = 2 * HEAD_DIM
ROW_TILE = 512
DOT_COLS = 256
CONV_TILE = 256
CF_HALO = 32
SC_HALO = 8
VMEM_LIMIT = 56 * 1024 * 1024

F32 = jnp.float32
BF16 = jnp.bfloat16


def _dot(a, b):
    return jnp.dot(a.astype(BF16), b.astype(BF16), preferred_element_type=F32)


def _dot_nt(a, b):
    return lax.dot_general(a.astype(BF16), b.astype(BF16), (((1,), (1,)), ((), ())),
                           preferred_element_type=F32)


def _dot_tn(a, b):
    return lax.dot_general(a.astype(BF16), b.astype(BF16), (((0,), (0,)), ((), ())),
                           preferred_element_type=F32)


def _rms(x, g):
    return x * lax.rsqrt(jnp.mean(x * x, axis=-1, keepdims=True) + NORM_EPS) * g


def _inproj_kernel(x_ref, g_ref, w_ref, o_ref):
    h = _rms(x_ref[...], g_ref[...]).astype(BF16)
    for j in range(PROJ_WIDTH // DOT_COLS):
        cols = slice(j * DOT_COLS, (j + 1) * DOT_COLS)
        o_ref[:, cols] = jnp.dot(h, w_ref[:, cols], preferred_element_type=F32)


def _inproj(x2, g, w):
    rows = x2.shape[0]
    return pl.pallas_call(
        _inproj_kernel,
        grid=(rows // ROW_TILE,),
        in_specs=[
            pl.BlockSpec((ROW_TILE, D_MODEL), lambda i: (i, 0)),
            pl.BlockSpec((1, D_MODEL), lambda i: (0, 0)),
            pl.BlockSpec((D_MODEL, PROJ_WIDTH), lambda i: (0, 0)),
        ],
        out_specs=pl.BlockSpec((ROW_TILE, PROJ_WIDTH), lambda i: (i, 0)),
        out_shape=jax.ShapeDtypeStruct((rows, PROJ_WIDTH), F32),
        compiler_params=pltpu.CompilerParams(
            dimension_semantics=("parallel",), vmem_limit_bytes=VMEM_LIMIT),
        name="inproj",
    )(x2, g, w)


_MU_R, _MU_K, _MU_V, _W0, _A0, _K_K, _K_A, _R_K, _LNX_G, _LNX_B = range(10)
_PVEC_ROWS = 16


def _rwkv_kernel(r_ref, k_ref, v_ref, la_ref, lb_ref, pv_ref, lw_ref, o_ref):
    seq = r_ref.shape[0]
    n_chunks = seq // CHUNK
    pv = pv_ref[...]
    prow = lambda i: pv[i:i + 1, :]
    w2 = lw_ref[0:DECAY_RANK, :]
    a2 = lw_ref[DECAY_RANK:DECAY_RANK + ICLR_RANK, :]
    g2 = lw_ref[DECAY_RANK + ICLR_RANK:, :]

    row = lax.broadcasted_iota(jnp.int32, (CHUNK, CHUNK), 0)
    col = lax.broadcasted_iota(jnp.int32, (CHUNK, CHUNK), 1)
    tri_incl = (row >= col).astype(BF16)
    eye = row == col
    row2 = lax.broadcasted_iota(jnp.int32, (2 * CHUNK, 2 * CHUNK), 0)
    col2 = lax.broadcasted_iota(jnp.int32, (2 * CHUNK, 2 * CHUNK), 1) % CHUNK
    keep2 = jnp.where(row2 < CHUNK, row2, row2 - CHUNK + 1) > col2
    first_row = lax.broadcasted_iota(jnp.int32, (CHUNK, 1), 0) == 0
    head0 = lax.broadcasted_iota(jnp.int32, (CHUNK, PAIR), 1) < HEAD_DIM
    wlane = lax.broadcasted_iota(jnp.int32, (CHUNK, 3 * HEAD_DIM), 1) < HEAD_DIM

    def head_sum(x):
        s0 = jnp.sum(jnp.where(head0, x, 0.0), axis=-1, keepdims=True)
        s1 = jnp.sum(jnp.where(head0, 0.0, x), axis=-1, keepdims=True)
        return jnp.where(head0, s0, s1)

    def chunk_body(c, states):
        t0 = pl.multiple_of(c * CHUNK, CHUNK)
        rows = pl.ds(t0, CHUNK)
        prev = pl.ds(jnp.maximum(t0 - 1, 0), 1)
        has_prev = (c > 0).astype(F32)

        def with_shift(ref):
            cur = ref[rows, :]
            last = ref[prev, :] * has_prev
            return cur, jnp.where(first_row, last, pltpu.roll(cur, 1, 0))

        def mixed(ref, mu):
            cur, sh = with_shift(ref)
            return cur + (sh - cur) * mu

        r = mixed(r_ref, prow(_MU_R))
        k = mixed(k_ref, prow(_MU_K))
        v = mixed(v_ref, prow(_MU_V))
        _, lb_sh = with_shift(lb_ref)
        low = la_ref[rows, :] + lb_sh
        w_pre = prow(_W0) + _dot(jnp.tanh(low[:, :DECAY_RANK]), w2)
        w_log = jnp.minimum(w_pre, 0.0) - jnp.log1p(jnp.exp(-jnp.abs(w_pre))) - 0.5
        log_decay = -jnp.exp(w_log)
        iclr = jax.nn.sigmoid(prow(_A0) + _dot(low[:, DECAY_RANK:DECAY_RANK + ICLR_RANK], a2))
        gate = _dot(jax.nn.sigmoid(low[:, DECAY_RANK + ICLR_RANK:]), g2)

        kk = k * prow(_K_K)
        kk = kk / jnp.maximum(jnp.sqrt(head_sum(kk * kk)), 1e-12)
        k = k * (1.0 + (iclr - 1.0) * prow(_K_A))
        a_vec = -kk
        b_vec = kk * iclr

        ld_hi = log_decay.astype(BF16)
        ld_lo = (log_decay - ld_hi.astype(F32)).astype(BF16)
        cum = (jnp.dot(tri_incl, ld_hi, preferred_element_type=F32)
               + jnp.dot(tri_incl, ld_lo, preferred_element_type=F32))
        cum_end = cum[CHUNK - 1:CHUNK, :]
        e_incl = jnp.exp(cum)
        e_neg = jnp.exp(-cum)
        e_end = jnp.exp(cum_end - cum)
        a_t = a_vec * jnp.exp(cum - log_decay)
        r_t = r * e_incl
        b_t = b_vec * e_neg
        k_t = k * e_neg
        b_h = b_vec * e_end
        k_h = k * e_end
        p_end = jnp.exp(cum_end)

        ys, new_states = [], []
        for h in range(2):
            hs = slice(h * HEAD_DIM, (h + 1) * HEAD_DIM)
            st = states[h]
            at, rt, bt, kt, bh, kh, vh = (z[:, hs] for z in (a_t, r_t, b_t, k_t, b_h, k_h, v))
            big = _dot_nt(jnp.concatenate([at, rt], axis=0), jnp.concatenate([bt, kt], axis=0))
            big = jnp.where(keep2, big, 0.0)
            top, bot = big[:CHUNK], big[CHUNK:]
            w = jnp.concatenate([top, at], axis=1)
            for _ in range(6):
                res = _dot(w[:, :CHUNK], w)
                w = res + jnp.where(wlane, 0.0, w)
            a_p = w[:, PAIR:]
            zeros = jnp.zeros((CHUNK, HEAD_DIM), F32)
            u_v = _dot(w[:, :PAIR], jnp.concatenate([zeros, vh], axis=0))
            uv2 = jnp.concatenate([u_v, vh], axis=0)
            ap0 = jnp.concatenate([a_p, zeros], axis=0)
            r_p = rt + _dot(bot, ap0)
            y_v = _dot(bot, uv2)
            m = jnp.where(eye, p_end[:, hs], 0.0) + _dot_tn(bh, a_p)
            n = _dot_tn(jnp.concatenate([bh, kh], axis=0), uv2)
            out = _dot(jnp.concatenate([m, r_p], axis=0), st)
            new_states.append(out[:HEAD_DIM] + n)
            ys.append(out[HEAD_DIM:] + y_v)

        y = jnp.concatenate(ys, axis=1)
        mean = head_sum(y) * (1.0 / HEAD_DIM)
        yc = y - mean
        var = head_sum(yc * yc) * (1.0 / HEAD_DIM)
        y = yc * lax.rsqrt(var + GN_EPS) * prow(_LNX_G) + prow(_LNX_B)
        bonus = head_sum(r * k * prow(_R_K)) * v
        o_ref[rows, :] = ((y + bonus) * gate).astype(o_ref.dtype)
        return tuple(new_states)

    zero_state = jnp.zeros((HEAD_DIM, HEAD_DIM), F32)
    lax.fori_loop(0, n_chunks, chunk_body, (zero_state, zero_state))


def _rwkv(proj3, pvec, lowrank_w):
    bsz, seq, _ = proj3.shape
    n_pairs = RWKV_WIDTH // PAIR
    col = lambda off: pl.BlockSpec((None, seq, PAIR), lambda b, p, off=off: (b, 0, off + p))
    low = lambda blk: pl.BlockSpec((None, seq, LOW_WIDTH), lambda b, p, blk=blk: (b, 0, blk))
    return pl.pallas_call(
        _rwkv_kernel,
        grid=(bsz, n_pairs),
        in_specs=[
            col(0), col(n_pairs), col(2 * n_pairs),
            low(IN_WIDTH // LOW_WIDTH), low(IN_WIDTH // LOW_WIDTH + 1),
            pl.BlockSpec((_PVEC_ROWS, PAIR), lambda b, p: (0, p)),
            pl.BlockSpec((LOW_WIDTH, PAIR), lambda b, p: (0, p)),
        ],
        out_specs=pl.BlockSpec((None, seq, PAIR), lambda b, p: (b, 0, p)),
        out_shape=jax.ShapeDtypeStruct((bsz, seq, RWKV_WIDTH), BF16),
        compiler_params=pltpu.CompilerParams(
            dimension_semantics=("parallel", "parallel"), vmem_limit_bytes=VMEM_LIMIT),
        name="rwkv",
    )(proj3, proj3, proj3, proj3, proj3, pvec, lowrank_w)


def _conv_kernel(gb_ref, gc_ref, u_ref, val_ref, cg_ref, scw_ref, cfw_ref, cfp_ref, o_ref,
                 sc_buf, cf_buf):
    tile = gb_ref.shape[0]

    @pl.when(pl.program_id(1) == 0)
    def _():
        sc_buf[0:SC_HALO, :] = jnp.zeros((SC_HALO, SC_WIDTH), F32)
        cf_buf[0:CF_HALO, :] = jnp.zeros((CF_HALO, CF_WIDTH), F32)

    sc_buf[SC_HALO:, :] = gc_ref[...] * u_ref[...]
    acc = jnp.zeros((tile, SC_WIDTH), F32)
    for j in range(SC_KERNEL):
        acc = acc + scw_ref[j:j + 1, :] * sc_buf[pl.ds(SC_HALO - (SC_KERNEL - 1) + j, tile), :]
    o_ref[:, 0:SC_WIDTH] = (gb_ref[...] * acc).astype(o_ref.dtype)

    cf_buf[CF_HALO:, :] = val_ref[...] * jax.nn.sigmoid(cg_ref[...])
    acc = jnp.zeros((tile, CF_WIDTH), F32) + cfp_ref[0:1, :]
    for j in range(CF_KERNEL):
        acc = acc + cfw_ref[j:j + 1, :] * cf_buf[pl.ds(CF_HALO - (CF_KERNEL - 1) + j, tile), :]
    mu = jnp.mean(acc, axis=-1, keepdims=True)
    cen = acc - mu
    var = jnp.mean(cen * cen, axis=-1, keepdims=True)
    z = cen * lax.rsqrt(var + LN_EPS) * cfp_ref[1:2, :] + cfp_ref[2:3, :]
    o_ref[:, SC_WIDTH:] = (z * jax.nn.sigmoid(z)).astype(o_ref.dtype)

    sc_buf[0:SC_HALO, :] = sc_buf[tile:tile + SC_HALO, :]
    cf_buf[0:CF_HALO, :] = cf_buf[tile:tile + CF_HALO, :]


def _conv(proj3, sc_w, cf_w, cf_p):
    bsz, seq, _ = proj3.shape
    base = 3 * RWKV_WIDTH // SC_WIDTH
    col = lambda blk: pl.BlockSpec((None, CONV_TILE, SC_WIDTH), lambda b, t, blk=blk: (b, t, blk))
    full = lambda a: pl.BlockSpec(a.shape, lambda b, t: (0, 0))
    return pl.pallas_call(
        _conv_kernel,
        grid=(bsz, seq // CONV_TILE),
        in_specs=[col(base), col(base + 1), col(base + 2), col(base + 3), col(base + 4),
                  full(sc_w), full(cf_w), full(cf_p)],
        out_specs=pl.BlockSpec((None, CONV_TILE, SC_WIDTH + CF_WIDTH), lambda b, t: (b, t, 0)),
        out_shape=jax.ShapeDtypeStruct((bsz, seq, SC_WIDTH + CF_WIDTH), BF16),
        scratch_shapes=[pltpu.VMEM((SC_HALO + CONV_TILE, SC_WIDTH), F32),
                        pltpu.VMEM((CF_HALO + CONV_TILE, CF_WIDTH), F32)],
        compiler_params=pltpu.CompilerParams(
            dimension_semantics=("parallel", "arbitrary"), vmem_limit_bytes=VMEM_LIMIT),
        name="conv",
    )(proj3, proj3, proj3, proj3, proj3, sc_w, cf_w, cf_p)


def _outffn_kernel(x_ref, yr_ref, yc_ref, wo_ref, wg_ref, wu_ref, wd_ref, g_ref, o_ref):
    mix = (jnp.dot(yr_ref[...], wo_ref[0:RWKV_WIDTH, :], preferred_element_type=F32)
           + jnp.dot(yc_ref[...], wo_ref[RWKV_WIDTH:, :], preferred_element_type=F32))
    x = x_ref[...] + _rms(mix, g_ref[0:1, :])
    h2 = _rms(x, g_ref[1:2, :]).astype(BF16)
    gate = jnp.dot(h2, wg_ref[...], preferred_element_type=F32)
    up = jnp.dot(h2, wu_ref[...], preferred_element_type=F32)
    act = (gate * jax.nn.sigmoid(gate) * up).astype(BF16)
    f = jnp.dot(act, wd_ref[...], preferred_element_type=F32)
    o_ref[...] = x + _rms(f, g_ref[2:3, :])


def _outffn(x2, y_rwkv, y_conv, wo, wg, wu, wd, gains):
    rows = x2.shape[0]
    tile = lambda a: pl.BlockSpec((ROW_TILE, a.shape[1]), lambda i: (i, 0))
    full = lambda a: pl.BlockSpec(a.shape, lambda i: (0, 0), pipeline_mode=pl.Buffered(1))
    return pl.pallas_call(
        _outffn_kernel,
        grid=(rows // ROW_TILE,),
        in_specs=[tile(x2), tile(y_rwkv), tile(y_conv),
                  full(wo), full(wg), full(wu), full(wd), full(gains)],
        out_specs=tile(x2),
        out_shape=jax.ShapeDtypeStruct(x2.shape, F32),
        compiler_params=pltpu.CompilerParams(
            dimension_semantics=("parallel",), vmem_limit_bytes=VMEM_LIMIT),
        name="outffn",
    )(x2, y_rwkv, y_conv, wo, wg, wu, wd, gains)


def _fold_low_rank(mu, w):
    return (1.0 - mu)[:, None] * w, mu[:, None] * w


def kernel(x, w_in, mu_rkv, mu_wag, w0, w1, w2, a0, a1, a2, g1, g2, k_k, k_a, r_k, lnx_g, lnx_b,
           sc_conv_w, cf_conv_w, cf_conv_b, cf_ln_g, cf_ln_b, w_o, w_gate, w_up, w_down,
           pre_mix_g, post_mix_g, pre_ffn_g, post_ffn_g):
    bsz, seq, d = x.shape
    depth = w_in.shape[0]
    x2 = x.reshape(bsz * seq, d)
    for l in range(depth):
        w1a, w1b = _fold_low_rank(mu_wag[l, 0], w1[l])
        a1a, a1b = _fold_low_rank(mu_wag[l, 1], a1[l])
        g1a, g1b = _fold_low_rank(mu_wag[l, 2], g1[l])
        w_cat = jnp.concatenate([w_in[l], w1a, a1a, g1a, w1b, a1b, g1b], axis=1).astype(BF16)
        pvec = jnp.concatenate([
            mu_rkv[l].reshape(3, RWKV_WIDTH),
            jnp.stack([w0[l], a0[l], k_k[l], k_a[l], r_k[l].reshape(-1), lnx_g[l], lnx_b[l]]),
            jnp.zeros((_PVEC_ROWS - 10, RWKV_WIDTH), F32)], axis=0)
        lowrank_w = jnp.concatenate([w2[l], a2[l], g2[l]], axis=0).astype(BF16)
        cf_p = jnp.stack([cf_conv_b[l], cf_ln_g[l], cf_ln_b[l]])
        gains = jnp.stack([post_mix_g[l], pre_ffn_g[l], post_ffn_g[l]])

        proj = _inproj(x2, pre_mix_g[l][None, :], w_cat)
        proj3 = proj.reshape(bsz, seq, PROJ_WIDTH)
        y_rwkv = _rwkv(proj3, pvec, lowrank_w).reshape(bsz * seq, RWKV_WIDTH)
        y_conv = _conv(proj3, sc_conv_w[l], cf_conv_w[l], cf_p).reshape(bsz * seq, SC_WIDTH + CF_WIDTH)
        x2 = _outffn(x2, y_rwkv, y_conv, w_o[l].astype(BF16), w_gate[l].astype(BF16),
                     w_up[l].astype(BF16), w_down[l].astype(BF16), gains)
    return x2.reshape(bsz, seq, d)
```

```python
import jax
import jax.numpy as jnp
from jax import lax
from jax.experimental import pallas as pl
from jax.experimental.pallas import tpu as pltpu

D_MODEL = 1024
HEAD_DIM = 64
RWKV_WIDTH = 512
SC_WIDTH = 256
CF_WIDTH = 256
SC_KERNEL = 3
CF_KERNEL = 31
DECAY_RANK = 64
ICLR_RANK = 64
GATE_RANK = 128
LOW_WIDTH = DECAY_RANK + ICLR_RANK + GATE_RANK
IN_WIDTH = 3 * RWKV_WIDTH + 3 * SC_WIDTH + 2 * CF_WIDTH
PROJ_WIDTH = IN_WIDTH + 2 * LOW_WIDTH
NORM_EPS = 1e-6
LN_EPS = 1e-5
GN_EPS = 64e-5

CHUNK = 64
PAIR = 2 * HEAD_DIM
SLAB_CHUNKS = 4
ROW_TILE = 512
DOT_COLS = 256
CONV_TILE = 256
CF_HALO = 32
SC_HALO = 8
VMEM_LIMIT = 56 * 1024 * 1024

F32 = jnp.float32
BF16 = jnp.bfloat16


def _dot(a, b):
    return jnp.dot(a.astype(BF16), b.astype(BF16), preferred_element_type=F32)


def _dot_nt(a, b):
    return lax.dot_general(a.astype(BF16), b.astype(BF16), (((1,), (1,)), ((), ())),
                           preferred_element_type=F32)


def _dot_tn(a, b):
    return lax.dot_general(a.astype(BF16), b.astype(BF16), (((0,), (0,)), ((), ())),
                           preferred_element_type=F32)


def _rms(x, g):
    return x * lax.rsqrt(jnp.mean(x * x, axis=-1, keepdims=True) + NORM_EPS) * g


def _inproj_kernel(x_ref, g_ref, w_ref, o_ref):
    h = _rms(x_ref[...], g_ref[...]).astype(BF16)
    for j in range(PROJ_WIDTH // DOT_COLS):
        cols = slice(j * DOT_COLS, (j + 1) * DOT_COLS)
        o_ref[:, cols] = jnp.dot(h, w_ref[:, cols], preferred_element_type=F32)


def _inproj(x2, g, w):
    rows = x2.shape[0]
    return pl.pallas_call(
        _inproj_kernel,
        grid=(rows // ROW_TILE,),
        in_specs=[
            pl.BlockSpec((ROW_TILE, D_MODEL), lambda i: (i, 0)),
            pl.BlockSpec((1, D_MODEL), lambda i: (0, 0)),
            pl.BlockSpec((D_MODEL, PROJ_WIDTH), lambda i: (0, 0)),
        ],
        out_specs=pl.BlockSpec((ROW_TILE, PROJ_WIDTH), lambda i: (i, 0)),
        out_shape=jax.ShapeDtypeStruct((rows, PROJ_WIDTH), F32),
        compiler_params=pltpu.CompilerParams(
            dimension_semantics=("parallel",), vmem_limit_bytes=VMEM_LIMIT),
        name="inproj",
    )(x2, g, w)


_MU_R, _MU_K, _MU_V, _W0, _A0, _K_K, _K_A, _R_K, _LNX_G, _LNX_B = range(10)
_PVEC_ROWS = 16


def _rwkv_kernel(r_ref, k_ref, v_ref, la_ref, lb_ref, pv_ref, lw_ref, o_ref):
    seq = r_ref.shape[0]
    slab = SLAB_CHUNKS * CHUNK
    pv = pv_ref[...]
    prow = lambda i: pv[i:i + 1, :]
    w2 = lw_ref[0:DECAY_RANK, :]
    a2 = lw_ref[DECAY_RANK:DECAY_RANK + ICLR_RANK, :]
    g2 = lw_ref[DECAY_RANK + ICLR_RANK:, :]

    iota = lambda shape, dim: lax.broadcasted_iota(jnp.int32, shape, dim)
    srow, scol = iota((slab, slab), 0), iota((slab, slab), 1)
    tri_incl = ((srow // CHUNK == scol // CHUNK) & (srow >= scol)).astype(BF16)
    first_row = iota((slab, 1), 0) == 0
    head0_s = iota((slab, PAIR), 1) < HEAD_DIM
    lane_c = iota((CHUNK, PAIR), 1)
    head_c = (lane_c < HEAD_DIM, lane_c >= HEAD_DIM)
    row4 = iota((4 * CHUNK, PAIR), 0) % (2 * CHUNK)
    col4 = iota((4 * CHUNK, PAIR), 1) % CHUNK
    keep4 = jnp.where(row4 < CHUNK, row4, row4 - CHUNK + 1) > col4
    wlane = iota((CHUNK, 2 * PAIR), 1) < CHUNK
    eye_p = iota((PAIR, PAIR), 0) == iota((PAIR, PAIR), 1)
    same_head = (iota((PAIR, 2 * PAIR), 0) // HEAD_DIM) == (iota((PAIR, 2 * PAIR), 1) % PAIR) // HEAD_DIM
    zeros_c = jnp.zeros((CHUNK, PAIR), F32)

    def head_sum(x):
        s0 = jnp.sum(jnp.where(head0_s, x, 0.0), axis=-1, keepdims=True)
        s1 = jnp.sum(jnp.where(head0_s, 0.0, x), axis=-1, keepdims=True)
        return jnp.where(head0_s, s0, s1)

    def slab_body(i, state):
        t0 = pl.multiple_of(i * slab, slab)
        rows = pl.ds(t0, slab)
        prev = pl.ds(jnp.maximum(t0 - 1, 0), 1)
        has_prev = (i > 0).astype(F32)

        def with_shift(ref):
            cur = ref[rows, :]
            last = ref[prev, :] * has_prev
            return cur, jnp.where(first_row, last, pltpu.roll(cur, 1, 0))

        def mixed(ref, mu):
            cur, sh = with_shift(ref)
            return cur + (sh - cur) * mu

        r = mixed(r_ref, prow(_MU_R))
        k = mixed(k_ref, prow(_MU_K))
        v = mixed(v_ref, prow(_MU_V))
        _, lb_sh = with_shift(lb_ref)
        low = la_ref[rows, :] + lb_sh
        w_pre = prow(_W0) + _dot(jnp.tanh(low[:, :DECAY_RANK]), w2)
        w_log = jnp.minimum(w_pre, 0.0) - jnp.log1p(jnp.exp(-jnp.abs(w_pre))) - 0.5
        log_decay = -jnp.exp(w_log)
        iclr = jax.nn.sigmoid(prow(_A0) + _dot(low[:, DECAY_RANK:DECAY_RANK + ICLR_RANK], a2))
        gate = _dot(jax.nn.sigmoid(low[:, DECAY_RANK + ICLR_RANK:]), g2)

        kk = k * prow(_K_K)
        kk = kk / jnp.maximum(jnp.sqrt(head_sum(kk * kk)), 1e-12)
        k = k * (1.0 + (iclr - 1.0) * prow(_K_A))
        b_vec = kk * iclr

        ld_hi = log_decay.astype(BF16)
        ld_lo = (log_decay - ld_hi.astype(F32)).astype(BF16)
        cum = (jnp.dot(tri_incl, ld_hi, preferred_element_type=F32)
               + jnp.dot(tri_incl, ld_lo, preferred_element_type=F32))
        e_neg = jnp.exp(-cum)
        a_t = -kk * jnp.exp(cum - log_decay)
        r_t = r * jnp.exp(cum)
        b_t = b_vec * e_neg
        k_t = k * e_neg

        chunks = range(SLAB_CHUNKS)
        heads = range(2)
        cs = [slice(j * CHUNK, (j + 1) * CHUNK) for j in chunks]
        cum_end = [cum[(j + 1) * CHUNK - 1:(j + 1) * CHUNK, :] for j in chunks]
        at = [[jnp.where(head_c[h], a_t[cs[j]], 0.0) for h in heads] for j in chunks]
        rt = [[jnp.where(head_c[h], r_t[cs[j]], 0.0) for h in heads] for j in chunks]
        vh = [[jnp.where(head_c[h], v[cs[j]], 0.0) for h in heads] for j in chunks]
        big = [jnp.where(keep4,
                         _dot_nt(jnp.concatenate([at[j][0], rt[j][0], at[j][1], rt[j][1]], axis=0),
                                 jnp.concatenate([b_t[cs[j]], k_t[cs[j]]], axis=0)), 0.0)
               for j in chunks]
        bot = [[big[j][(2 * h + 1) * CHUNK:(2 * h + 2) * CHUNK] for h in heads] for j in chunks]
        w = [[jnp.concatenate([big[j][2 * h * CHUNK:(2 * h + 1) * CHUNK], at[j][h]], axis=1)
              for h in heads] for j in chunks]
        for _ in range(6):
            res = [[_dot(w[j][h][:, :CHUNK], w[j][h]) for h in heads] for j in chunks]
            w = [[res[j][h] + jnp.where(wlane, 0.0, w[j][h]) for h in heads] for j in chunks]
        u_v = [[_dot(w[j][h][:, :PAIR], jnp.concatenate([zeros_c, vh[j][h]], axis=0))
                for h in heads] for j in chunks]
        ry = [[_dot(bot[j][h],
                    jnp.concatenate([jnp.concatenate([w[j][h][:, PAIR:], u_v[j][h]], axis=1),
                                     jnp.concatenate([zeros_c, vh[j][h]], axis=1)], axis=0))
               for h in heads] for j in chunks]
        mn = []
        for j in chunks:
            e_end = jnp.exp(cum_end[j] - cum[cs[j]])
            bk_h = jnp.concatenate([b_vec[cs[j]] * e_end, k[cs[j]] * e_end], axis=0)
            apuv = jnp.concatenate(
                [jnp.concatenate([w[j][0][:, PAIR:] + w[j][1][:, PAIR:], u_v[j][0] + u_v[j][1]], axis=1),
                 jnp.concatenate([zeros_c, v[cs[j]]], axis=1)], axis=0)
            mn.append(jnp.where(same_head, _dot_tn(bk_h, apuv), 0.0))
        ys = []
        for j in chunks:
            m = mn[j][:, :PAIR] + jnp.where(eye_p, jnp.exp(cum_end[j]), 0.0)
            r_p = r_t[cs[j]] + ry[j][0][:, :PAIR] + ry[j][1][:, :PAIR]
            out = _dot(jnp.concatenate([m, r_p], axis=0), state)
            state = out[:PAIR] + mn[j][:, PAIR:]
            ys.append(out[PAIR:] + ry[j][0][:, PAIR:] + ry[j][1][:, PAIR:])

        y = jnp.concatenate(ys, axis=0)
        mean = head_sum(y) * (1.0 / HEAD_DIM)
        yc = y - mean
        var = head_sum(yc * yc) * (1.0 / HEAD_DIM)
        y = yc * lax.rsqrt(var + GN_EPS) * prow(_LNX_G) + prow(_LNX_B)
        bonus = head_sum(r * k * prow(_R_K)) * v
        o_ref[rows, :] = ((y + bonus) * gate).astype(o_ref.dtype)
        return state

    lax.fori_loop(0, seq // slab, slab_body, jnp.zeros((PAIR, PAIR), F32))


def _rwkv(proj3, pvec, lowrank_w):
    bsz, seq, _ = proj3.shape
    n_pairs = RWKV_WIDTH // PAIR
    col = lambda off: pl.BlockSpec((None, seq, PAIR), lambda b, p, off=off: (b, 0, off + p))
    low = lambda blk: pl.BlockSpec((None, seq, LOW_WIDTH), lambda b, p, blk=blk: (b, 0, blk))
    return pl.pallas_call(
        _rwkv_kernel,
        grid=(bsz, n_pairs),
        in_specs=[
            col(0), col(n_pairs), col(2 * n_pairs),
            low(IN_WIDTH // LOW_WIDTH), low(IN_WIDTH // LOW_WIDTH + 1),
            pl.BlockSpec((_PVEC_ROWS, PAIR), lambda b, p: (0, p)),
            pl.BlockSpec((LOW_WIDTH, PAIR), lambda b, p: (0, p)),
        ],
        out_specs=pl.BlockSpec((None, seq, PAIR), lambda b, p: (b, 0, p)),
        out_shape=jax.ShapeDtypeStruct((bsz, seq, RWKV_WIDTH), BF16),
        compiler_params=pltpu.CompilerParams(
            dimension_semantics=("parallel", "parallel"), vmem_limit_bytes=VMEM_LIMIT),
        name="rwkv",
    )(proj3, proj3, proj3, proj3, proj3, pvec, lowrank_w)


def _conv_kernel(gb_ref, gc_ref, u_ref, val_ref, cg_ref, scw_ref, cfw_ref, cfp_ref, o_ref,
                 sc_buf, cf_buf):
    tile = gb_ref.shape[0]

    @pl.when(pl.program_id(1) == 0)
    def _():
        sc_buf[0:SC_HALO, :] = jnp.zeros((SC_HALO, SC_WIDTH), F32)
        cf_buf[0:CF_HALO, :] = jnp.zeros((CF_HALO, CF_WIDTH), F32)

    sc_buf[SC_HALO:, :] = gc_ref[...] * u_ref[...]
    acc = jnp.zeros((tile, SC_WIDTH), F32)
    for j in range(SC_KERNEL):
        acc = acc + scw_ref[j:j + 1, :] * sc_buf[pl.ds(SC_HALO - (SC_KERNEL - 1) + j, tile), :]
    o_ref[:, 0:SC_WIDTH] = (gb_ref[...] * acc).astype(o_ref.dtype)

    cf_buf[CF_HALO:, :] = val_ref[...] * jax.nn.sigmoid(cg_ref[...])
    acc = jnp.zeros((tile, CF_WIDTH), F32) + cfp_ref[0:1, :]
    for j in range(CF_KERNEL):
        acc = acc + cfw_ref[j:j + 1, :] * cf_buf[pl.ds(CF_HALO - (CF_KERNEL - 1) + j, tile), :]
    mu = jnp.mean(acc, axis=-1, keepdims=True)
    cen = acc - mu
    var = jnp.mean(cen * cen, axis=-1, keepdims=True)
    z = cen * lax.rsqrt(var + LN_EPS) * cfp_ref[1:2, :] + cfp_ref[2:3, :]
    o_ref[:, SC_WIDTH:] = (z * jax.nn.sigmoid(z)).astype(o_ref.dtype)

    sc_buf[0:SC_HALO, :] = sc_buf[tile:tile + SC_HALO, :]
    cf_buf[0:CF_HALO, :] = cf_buf[tile:tile + CF_HALO, :]


def _conv(proj3, sc_w, cf_w, cf_p):
    bsz, seq, _ = proj3.shape
    base = 3 * RWKV_WIDTH // SC_WIDTH
    col = lambda blk: pl.BlockSpec((None, CONV_TILE, SC_WIDTH), lambda b, t, blk=blk: (b, t, blk))
    full = lambda a: pl.BlockSpec(a.shape, lambda b, t: (0, 0))
    return pl.pallas_call(
        _conv_kernel,
        grid=(bsz, seq // CONV_TILE),
        in_specs=[col(base), col(base + 1), col(base + 2), col(base + 3), col(base + 4),
                  full(sc_w), full(cf_w), full(cf_p)],
        out_specs=pl.BlockSpec((None, CONV_TILE, SC_WIDTH + CF_WIDTH), lambda b, t: (b, t, 0)),
        out_shape=jax.ShapeDtypeStruct((bsz, seq, SC_WIDTH + CF_WIDTH), BF16),
        scratch_shapes=[pltpu.VMEM((SC_HALO + CONV_TILE, SC_WIDTH), F32),
                        pltpu.VMEM((CF_HALO + CONV_TILE, CF_WIDTH), F32)],
        compiler_params=pltpu.CompilerParams(
            dimension_semantics=("parallel", "arbitrary"), vmem_limit_bytes=VMEM_LIMIT),
        name="conv",
    )(proj3, proj3, proj3, proj3, proj3, sc_w, cf_w, cf_p)


def _outffn_kernel(x_ref, yr_ref, yc_ref, wo_ref, wg_ref, wu_ref, wd_ref, g_ref, o_ref):
    mix = (jnp.dot(yr_ref[...], wo_ref[0:RWKV_WIDTH, :], preferred_element_type=F32)
           + jnp.dot(yc_ref[...], wo_ref[RWKV_WIDTH:, :], preferred_element_type=F32))
    x = x_ref[...] + _rms(mix, g_ref[0:1, :])
    h2 = _rms(x, g_ref[1:2, :]).astype(BF16)
    gate = jnp.dot(h2, wg_ref[...], preferred_element_type=F32)
    up = jnp.dot(h2, wu_ref[...], preferred_element_type=F32)
    act = (gate * jax.nn.sigmoid(gate) * up).astype(BF16)
    f = jnp.dot(act, wd_ref[...], preferred_element_type=F32)
    o_ref[...] = x + _rms(f, g_ref[2:3, :])


def _outffn(x2, y_rwkv, y_conv, wo, wg, wu, wd, gains):
    rows = x2.shape[0]
    tile = lambda a: pl.BlockSpec((ROW_TILE, a.shape[1]), lambda i: (i, 0))
    full = lambda a: pl.BlockSpec(a.shape, lambda i: (0, 0), pipeline_mode=pl.Buffered(1))
    return pl.pallas_call(
        _outffn_kernel,
        grid=(rows // ROW_TILE,),
        in_specs=[tile(x2), tile(y_rwkv), tile(y_conv),
                  full(wo), full(wg), full(wu), full(wd), full(gains)],
        out_specs=tile(x2),
        out_shape=jax.ShapeDtypeStruct(x2.shape, F32),
        compiler_params=pltpu.CompilerParams(
            dimension_semantics=("parallel",), vmem_limit_bytes=VMEM_LIMIT),
        name="outffn",
    )(x2, y_rwkv, y_conv, wo, wg, wu, wd, gains)


def _fold_low_rank(mu, w):
    return (1.0 - mu)[:, None] * w, mu[:, None] * w


def kernel(x, w_in, mu_rkv, mu_wag, w0, w1, w2, a0, a1, a2, g1, g2, k_k, k_a, r_k, lnx_g, lnx_b,
           sc_conv_w, cf_conv_w, cf_conv_b, cf_ln_g, cf_ln_b, w_o, w_gate, w_up, w_down,
           pre_mix_g, post_mix_g, pre_ffn_g, post_ffn_g):
    bsz, seq, d = x.shape
    depth = w_in.shape[0]
    x2 = x.reshape(bsz * seq, d)
    for l in range(depth):
        w1a, w1b = _fold_low_rank(mu_wag[l, 0], w1[l])
        a1a, a1b = _fold_low_rank(mu_wag[l, 1], a1[l])
        g1a, g1b = _fold_low_rank(mu_wag[l, 2], g1[l])
        w_cat = jnp.concatenate([w_in[l], w1a, a1a, g1a, w1b, a1b, g1b], axis=1).astype(BF16)
        pvec = jnp.concatenate([
            mu_rkv[l].reshape(3, RWKV_WIDTH),
            jnp.stack([w0[l], a0[l], k_k[l], k_a[l], r_k[l].reshape(-1), lnx_g[l], lnx_b[l]]),
            jnp.zeros((_PVEC_ROWS - 10, RWKV_WIDTH), F32)], axis=0)
        lowrank_w = jnp.concatenate([w2[l], a2[l], g2[l]], axis=0).astype(BF16)
        cf_p = jnp.stack([cf_conv_b[l], cf_ln_g[l], cf_ln_b[l]])
        gains = jnp.stack([post_mix_g[l], pre_ffn_g[l], post_ffn_g[l]])

        proj = _inproj(x2, pre_mix_g[l][None, :], w_cat)
        proj3 = proj.reshape(bsz, seq, PROJ_WIDTH)
        y_rwkv = _rwkv(proj3, pvec, lowrank_w).reshape(bsz * seq, RWKV_WIDTH)
        y_conv = _conv(proj3, sc_conv_w[l], cf_conv_w[l], cf_p).reshape(bsz * seq, SC_WIDTH + CF_WIDTH)
        x2 = _outffn(x2, y_rwkv, y_conv, w_o[l].astype(BF16), w_gate[l].astype(BF16),
                     w_up[l].astype(BF16), w_down[l].astype(BF16), gains)
    return x2.reshape(bsz, seq, d)
```

```python
import jax
import jax.numpy as jnp
from jax import lax
from jax.experimental import pallas as pl
from jax.experimental.pallas import tpu as pltpu

D_MODEL = 1024
HEAD_DIM = 64
RWKV_WIDTH = 512
SC_WIDTH = 256
CF_WIDTH = 256
SC_KERNEL = 3
CF_KERNEL = 31
DECAY_RANK = 64
ICLR_RANK = 64
GATE_RANK = 128
LOW_WIDTH = DECAY_RANK + ICLR_RANK + GATE_RANK
IN_WIDTH = 3 * RWKV_WIDTH + 3 * SC_WIDTH + 2 * CF_WIDTH
PROJ_WIDTH = IN_WIDTH + 2 * LOW_WIDTH
NORM_EPS = 1e-6
LN_EPS = 1e-5
GN_EPS = 64e-5

CHUNK = 64
PAIR = 2 * HEAD_DIM
SLAB_CHUNKS = 2
ROW_TILE = 512
DOT_COLS = 256
CONV_TILE = 256
CF_HALO = 32
SC_HALO = 8
VMEM_LIMIT = 56 * 1024 * 1024

F32 = jnp.float32
BF16 = jnp.bfloat16


def _dot(a, b):
    return jnp.dot(a.astype(BF16), b.astype(BF16), preferred_element_type=F32)


def _dot_nt(a, b):
    return lax.dot_general(a.astype(BF16), b.astype(BF16), (((1,), (1,)), ((), ())),
                           preferred_element_type=F32)


def _dot_tn(a, b):
    return lax.dot_general(a.astype(BF16), b.astype(BF16), (((0,), (0,)), ((), ())),
                           preferred_element_type=F32)


def _rms(x, g):
    return x * lax.rsqrt(jnp.mean(x * x, axis=-1, keepdims=True) + NORM_EPS) * g


def _inproj_kernel(x_ref, g_ref, w_ref, o_ref):
    h = _rms(x_ref[...], g_ref[...]).astype(BF16)
    for j in range(PROJ_WIDTH // DOT_COLS):
        cols = slice(j * DOT_COLS, (j + 1) * DOT_COLS)
        o_ref[:, cols] = jnp.dot(h, w_ref[:, cols], preferred_element_type=F32)


def _inproj(x2, g, w):
    rows = x2.shape[0]
    return pl.pallas_call(
        _inproj_kernel,
        grid=(rows // ROW_TILE,),
        in_specs=[
            pl.BlockSpec((ROW_TILE, D_MODEL), lambda i: (i, 0)),
            pl.BlockSpec((1, D_MODEL), lambda i: (0, 0)),
            pl.BlockSpec((D_MODEL, PROJ_WIDTH), lambda i: (0, 0)),
        ],
        out_specs=pl.BlockSpec((ROW_TILE, PROJ_WIDTH), lambda i: (i, 0)),
        out_shape=jax.ShapeDtypeStruct((rows, PROJ_WIDTH), F32),
        compiler_params=pltpu.CompilerParams(
            dimension_semantics=("parallel",), vmem_limit_bytes=VMEM_LIMIT),
        name="inproj",
    )(x2, g, w)


_MU_R, _MU_K, _MU_V, _W0, _A0, _K_K, _K_A, _R_K, _LNX_G, _LNX_B = range(10)
_PVEC_ROWS = 16


def _rwkv_kernel(r_ref, k_ref, v_ref, la_ref, lb_ref, pv_ref, lw_ref, o_ref):
    seq = r_ref.shape[0]
    slab = SLAB_CHUNKS * CHUNK
    n_pairs = RWKV_WIDTH // PAIR
    pv = pv_ref[...]
    prow = lambda i: pv[i:i + 1, :]
    w2 = lw_ref[0:DECAY_RANK, :]
    a2 = lw_ref[DECAY_RANK:DECAY_RANK + ICLR_RANK, :]
    g2 = lw_ref[DECAY_RANK + ICLR_RANK:, :]

    iota = lambda shape, dim: lax.broadcasted_iota(jnp.int32, shape, dim)
    srow, scol = iota((slab, slab), 0), iota((slab, slab), 1)
    tri_incl = ((srow // CHUNK == scol // CHUNK) & (srow >= scol)).astype(BF16)
    first_row = iota((slab, 1), 0) == 0
    head0 = iota((1, PAIR), 1) < HEAD_DIM
    head_c = (head0, jnp.logical_not(head0))
    row4 = iota((4 * CHUNK, PAIR), 0) % (2 * CHUNK)
    col4 = iota((4 * CHUNK, PAIR), 1) % CHUNK
    keep4 = jnp.where(row4 < CHUNK, row4, row4 - CHUNK + 1) > col4
    plane = iota((1, PAIR), 1) < CHUNK
    eye_p = iota((PAIR, PAIR), 0) == iota((PAIR, PAIR), 1)
    same_head = (iota((PAIR, 2 * PAIR), 0) // HEAD_DIM) == (iota((PAIR, 2 * PAIR), 1) % PAIR) // HEAD_DIM
    zeros_c = jnp.zeros((CHUNK, PAIR), F32)

    def head_sum(x):
        outs = []
        for p in range(x.shape[1] // PAIR):
            xp = x[:, p * PAIR:(p + 1) * PAIR]
            s0 = jnp.sum(jnp.where(head0, xp, 0.0), axis=-1, keepdims=True)
            s1 = jnp.sum(jnp.where(head0, 0.0, xp), axis=-1, keepdims=True)
            outs.append(jnp.where(head0, s0, s1))
        return outs[0] if len(outs) == 1 else jnp.concatenate(outs, axis=1)

    def slab_body(i, states):
        t0 = pl.multiple_of(i * slab, slab)
        rows = pl.ds(t0, slab)
        prev = pl.ds(jnp.maximum(t0 - 1, 0), 1)
        has_prev = jnp.where(i > 0, 1.0, 0.0).astype(F32)

        def with_shift(ref):
            cur = ref[rows, :]
            last = ref[prev, :] * has_prev
            return cur, jnp.where(first_row, last, pltpu.roll(cur, 1, 0))

        def mixed(ref, mu):
            cur, sh = with_shift(ref)
            return cur + (sh - cur) * mu

        r = mixed(r_ref, prow(_MU_R))
        k = mixed(k_ref, prow(_MU_K))
        v = mixed(v_ref, prow(_MU_V))
        _, lb_sh = with_shift(lb_ref)
        low = la_ref[rows, :] + lb_sh
        w_pre = prow(_W0) + _dot(jnp.tanh(low[:, :DECAY_RANK]), w2)
        w_log = jnp.minimum(w_pre, 0.0) - jnp.log1p(jnp.exp(-jnp.abs(w_pre))) - 0.5
        log_decay = -jnp.exp(w_log)
        iclr = jax.nn.sigmoid(prow(_A0) + _dot(low[:, DECAY_RANK:DECAY_RANK + ICLR_RANK], a2))
        gate = _dot(jax.nn.sigmoid(low[:, DECAY_RANK + ICLR_RANK:]), g2)

        kk = k * prow(_K_K)
        kk = kk / jnp.maximum(jnp.sqrt(head_sum(kk * kk)), 1e-12)
        k = k * (1.0 + (iclr - 1.0) * prow(_K_A))
        b_vec = kk * iclr

        ld_hi = log_decay.astype(BF16)
        ld_lo = (log_decay - ld_hi.astype(F32)).astype(BF16)
        cum = (jnp.dot(tri_incl, ld_hi, preferred_element_type=F32)
               + jnp.dot(tri_incl, ld_lo, preferred_element_type=F32))
        e_neg = jnp.exp(-cum)
        a_t = -kk * jnp.exp(cum - log_decay)
        r_t = r * jnp.exp(cum)
        b_t = b_vec * e_neg
        k_t = k * e_neg
        bonus = head_sum(r * k * prow(_R_K)) * v

        units = [(j, p) for j in range(SLAB_CHUNKS) for p in range(n_pairs)]
        heads = range(2)
        blk = lambda x, u: x[u[0] * CHUNK:(u[0] + 1) * CHUNK, u[1] * PAIR:(u[1] + 1) * PAIR]
        cum_end = {u: cum[(u[0] + 1) * CHUNK - 1:(u[0] + 1) * CHUNK, u[1] * PAIR:(u[1] + 1) * PAIR]
                   for u in units}
        at = {u: [jnp.where(head_c[h], blk(a_t, u), 0.0) for h in heads] for u in units}
        rt = {u: [jnp.where(head_c[h], blk(r_t, u), 0.0) for h in heads] for u in units}
        vh = {u: [jnp.where(head_c[h], blk(v, u), 0.0) for h in heads] for u in units}
        big = {u: jnp.where(keep4,
                            _dot_nt(jnp.concatenate([at[u][0], rt[u][0], at[u][1], rt[u][1]], axis=0),
                                    jnp.concatenate([blk(b_t, u), blk(k_t, u)], axis=0)), 0.0)
               for u in units}
        bot = {u: [big[u][(2 * h + 1) * CHUNK:(2 * h + 2) * CHUNK] for h in heads] for u in units}
        px = {u: [big[u][2 * h * CHUNK:(2 * h + 1) * CHUNK] for h in heads] for u in units}
        xa = at
        for _ in range(6):
            res = {u: [_dot(px[u][h][:, :CHUNK], jnp.concatenate([px[u][h], xa[u][h]], axis=1))
                       for h in heads] for u in units}
            px = {u: [res[u][h][:, :PAIR] + jnp.where(plane, 0.0, px[u][h]) for h in heads] for u in units}
            xa = {u: [res[u][h][:, PAIR:] + xa[u][h] for h in heads] for u in units}
        u_v = {u: [_dot(px[u][h], jnp.concatenate([zeros_c, vh[u][h]], axis=0)) for h in heads]
               for u in units}
        ry = {u: [_dot(bot[u][h],
                       jnp.concatenate([jnp.concatenate([xa[u][h], u_v[u][h]], axis=1),
                                        jnp.concatenate([zeros_c, vh[u][h]], axis=1)], axis=0))
                  for h in heads] for u in units}
        mn = {}
        for u in units:
            e_end = jnp.exp(cum_end[u] - blk(cum, u))
            bk_h = jnp.concatenate([blk(b_vec, u) * e_end, blk(k, u) * e_end], axis=0)
            apuv = jnp.concatenate(
                [jnp.concatenate([xa[u][0] + xa[u][1], u_v[u][0] + u_v[u][1]], axis=1),
                 jnp.concatenate([zeros_c, blk(v, u)], axis=1)], axis=0)
            mn[u] = jnp.where(same_head, _dot_tn(bk_h, apuv), 0.0)
        states = list(states)
        for u in units:
            j, p = u
            m = mn[u][:, :PAIR] + jnp.where(eye_p, jnp.exp(cum_end[u]), 0.0)
            r_p = blk(r_t, u) + ry[u][0][:, :PAIR] + ry[u][1][:, :PAIR]
            out = _dot(jnp.concatenate([m, r_p], axis=0), states[p])
            states[p] = out[:PAIR] + mn[u][:, PAIR:]
            y = out[PAIR:] + ry[u][0][:, PAIR:] + ry[u][1][:, PAIR:]
            yc = y - head_sum(y) * (1.0 / HEAD_DIM)
            var = head_sum(yc * yc) * (1.0 / HEAD_DIM)
            lanes = slice(p * PAIR, (p + 1) * PAIR)
            y = yc * lax.rsqrt(var + GN_EPS) * prow(_LNX_G)[:, lanes] + prow(_LNX_B)[:, lanes]
            o_ref[pl.ds(t0 + j * CHUNK, CHUNK), lanes] = ((y + blk(bonus, u)) * blk(gate, u)).astype(o_ref.dtype)
        return tuple(states)

    zero_state = jnp.zeros((PAIR, PAIR), F32)
    lax.fori_loop(0, seq // slab, slab_body, (zero_state,) * n_pairs)


def _rwkv(proj3, pvec, lowrank_w):
    bsz, seq, _ = proj3.shape
    col = lambda blk: pl.BlockSpec((None, seq, RWKV_WIDTH), lambda b, blk=blk: (b, 0, blk))
    low = lambda blk: pl.BlockSpec((None, seq, LOW_WIDTH), lambda b, blk=blk: (b, 0, blk))
    full = lambda a: pl.BlockSpec(a.shape, lambda b: (0, 0))
    return pl.pallas_call(
        _rwkv_kernel,
        grid=(bsz,),
        in_specs=[col(0), col(1), col(2),
                  low(IN_WIDTH // LOW_WIDTH), low(IN_WIDTH // LOW_WIDTH + 1),
                  full(pvec), full(lowrank_w)],
        out_specs=pl.BlockSpec((None, seq, RWKV_WIDTH), lambda b: (b, 0, 0)),
        out_shape=jax.ShapeDtypeStruct((bsz, seq, RWKV_WIDTH), BF16),
        compiler_params=pltpu.CompilerParams(
            dimension_semantics=("parallel",), vmem_limit_bytes=VMEM_LIMIT),
        name="rwkv",
    )(proj3, proj3, proj3, proj3, proj3, pvec, lowrank_w)


def _conv_kernel(gb_ref, gc_ref, u_ref, val_ref, cg_ref, scw_ref, cfw_ref, cfp_ref, o_ref,
                 sc_buf, cf_buf):
    tile = gb_ref.shape[0]

    @pl.when(pl.program_id(1) == 0)
    def _():
        sc_buf[0:SC_HALO, :] = jnp.zeros((SC_HALO, SC_WIDTH), F32)
        cf_buf[0:CF_HALO, :] = jnp.zeros((CF_HALO, CF_WIDTH), F32)

    sc_buf[SC_HALO:, :] = gc_ref[...] * u_ref[...]
    acc = jnp.zeros((tile, SC_WIDTH), F32)
    for j in range(SC_KERNEL):
        acc = acc + scw_ref[j:j + 1, :] * sc_buf[pl.ds(SC_HALO - (SC_KERNEL - 1) + j, tile), :]
    o_ref[:, 0:SC_WIDTH] = (gb_ref[...] * acc).astype(o_ref.dtype)

    cf_buf[CF_HALO:, :] = val_ref[...] * jax.nn.sigmoid(cg_ref[...])
    acc = jnp.zeros((tile, CF_WIDTH), F32) + cfp_ref[0:1, :]
    for j in range(CF_KERNEL):
        acc = acc + cfw_ref[j:j + 1, :] * cf_buf[pl.ds(CF_HALO - (CF_KERNEL - 1) + j, tile), :]
    mu = jnp.mean(acc, axis=-1, keepdims=True)
    cen = acc - mu
    var = jnp.mean(cen * cen, axis=-1, keepdims=True)
    z = cen * lax.rsqrt(var + LN_EPS) * cfp_ref[1:2, :] + cfp_ref[2:3, :]
    o_ref[:, SC_WIDTH:] = (z * jax.nn.sigmoid(z)).astype(o_ref.dtype)

    sc_buf[0:SC_HALO, :] = sc_buf[tile:tile + SC_HALO, :]
    cf_buf[0:CF_HALO, :] = cf_buf[tile:tile + CF_HALO, :]


def _conv(proj3, sc_w, cf_w, cf_p):
    bsz, seq, _ = proj3.shape
    base = 3 * RWKV_WIDTH // SC_WIDTH
    col = lambda blk: pl.BlockSpec((None, CONV_TILE, SC_WIDTH), lambda b, t, blk=blk: (b, t, blk))
    full = lambda a: pl.BlockSpec(a.shape, lambda b, t: (0, 0))
    return pl.pallas_call(
        _conv_kernel,
        grid=(bsz, seq // CONV_TILE),
        in_specs=[col(base), col(base + 1), col(base + 2), col(base + 3), col(base + 4),
                  full(sc_w), full(cf_w), full(cf_p)],
        out_specs=pl.BlockSpec((None, CONV_TILE, SC_WIDTH + CF_WIDTH), lambda b, t: (b, t, 0)),
        out_shape=jax.ShapeDtypeStruct((bsz, seq, SC_WIDTH + CF_WIDTH), BF16),
        scratch_shapes=[pltpu.VMEM((SC_HALO + CONV_TILE, SC_WIDTH), F32),
                        pltpu.VMEM((CF_HALO + CONV_TILE, CF_WIDTH), F32)],
        compiler_params=pltpu.CompilerParams(
            dimension_semantics=("parallel", "arbitrary"), vmem_limit_bytes=VMEM_LIMIT),
        name="conv",
    )(proj3, proj3, proj3, proj3, proj3, sc_w, cf_w, cf_p)


def _outffn_kernel(x_ref, yr_ref, yc_ref, wo_ref, wg_ref, wu_ref, wd_ref, g_ref, o_ref):
    mix = (jnp.dot(yr_ref[...], wo_ref[0:RWKV_WIDTH, :], preferred_element_type=F32)
           + jnp.dot(yc_ref[...], wo_ref[RWKV_WIDTH:, :], preferred_element_type=F32))
    x = x_ref[...] + _rms(mix, g_ref[0:1, :])
    h2 = _rms(x, g_ref[1:2, :]).astype(BF16)
    gate = jnp.dot(h2, wg_ref[...], preferred_element_type=F32)
    up = jnp.dot(h2, wu_ref[...], preferred_element_type=F32)
    act = (gate * jax.nn.sigmoid(gate) * up).astype(BF16)
    f = jnp.dot(act, wd_ref[...], preferred_element_type=F32)
    o_ref[...] = x + _rms(f, g_ref[2:3, :])


def _outffn(x2, y_rwkv, y_conv, wo, wg, wu, wd, gains):
    rows = x2.shape[0]
    tile = lambda a: pl.BlockSpec((ROW_TILE, a.shape[1]), lambda i: (i, 0))
    full = lambda a: pl.BlockSpec(a.shape, lambda i: (0, 0), pipeline_mode=pl.Buffered(1))
    return pl.pallas_call(
        _outffn_kernel,
        grid=(rows // ROW_TILE,),
        in_specs=[tile(x2), tile(y_rwkv), tile(y_conv),
                  full(wo), full(wg), full(wu), full(wd), full(gains)],
        out_specs=tile(x2),
        out_shape=jax.ShapeDtypeStruct(x2.shape, F32),
        compiler_params=pltpu.CompilerParams(
            dimension_semantics=("parallel",), vmem_limit_bytes=VMEM_LIMIT),
        name="outffn",
    )(x2, y_rwkv, y_conv, wo, wg, wu, wd, gains)


def _fold_low_rank(mu, w):
    return (1.0 - mu)[:, None] * w, mu[:, None] * w


def kernel(x, w_in, mu_rkv, mu_wag, w0, w1, w2, a0, a1, a2, g1, g2, k_k, k_a, r_k, lnx_g, lnx_b,
           sc_conv_w, cf_conv_w, cf_conv_b, cf_ln_g, cf_ln_b, w_o, w_gate, w_up, w_down,
           pre_mix_g, post_mix_g, pre_ffn_g, post_ffn_g):
    bsz, seq, d = x.shape
    depth = w_in.shape[0]
    x2 = x.reshape(bsz * seq, d)
    for l in range(depth):
        w1a, w1b = _fold_low_rank(mu_wag[l, 0], w1[l])
        a1a, a1b = _fold_low_rank(mu_wag[l, 1], a1[l])
        g1a, g1b = _fold_low_rank(mu_wag[l, 2], g1[l])
        w_cat = jnp.concatenate([w_in[l], w1a, a1a, g1a, w1b, a1b, g1b], axis=1).astype(BF16)
        pvec = jnp.concatenate([
            mu_rkv[l].reshape(3, RWKV_WIDTH),
            jnp.stack([w0[l], a0[l], k_k[l], k_a[l], r_k[l].reshape(-1), lnx_g[l], lnx_b[l]]),
            jnp.zeros((_PVEC_ROWS - 10, RWKV_WIDTH), F32)], axis=0)
        lowrank_w = jnp.concatenate([w2[l], a2[l], g2[l]], axis=0).astype(BF16)
        cf_p = jnp.stack([cf_conv_b[l], cf_ln_g[l], cf_ln_b[l]])
        gains = jnp.stack([post_mix_g[l], pre_ffn_g[l], post_ffn_g[l]])

        proj = _inproj(x2, pre_mix_g[l][None, :], w_cat)
        proj3 = proj.reshape(bsz, seq, PROJ_WIDTH)
        y_rwkv = _rwkv(proj3, pvec, lowrank_w).reshape(bsz * seq, RWKV_WIDTH)
        y_conv = _conv(proj3, sc_conv_w[l], cf_conv_w[l], cf_p).reshape(bsz * seq, SC_WIDTH + CF_WIDTH)
        x2 = _outffn(x2, y_rwkv, y_conv, w_o[l].astype(BF16), w_gate[l].astype(BF16),
                     w_up[l].astype(BF16), w_down[l].astype(BF16), gains)
    return x2.reshape(bsz, seq, d)
```

```python
import jax
import jax.numpy as jnp
from jax import lax
from jax.experimental import pallas as pl
from jax.experimental.pallas import tpu as pltpu

D_MODEL = 1024
HEAD_DIM = 64
RWKV_WIDTH = 512
SC_WIDTH = 256
CF_WIDTH = 256
SC_KERNEL = 3
CF_KERNEL = 31
DECAY_RANK = 64
ICLR_RANK = 64
GATE_RANK = 128
LOW_WIDTH = DECAY_RANK + ICLR_RANK + GATE_RANK
IN_WIDTH = 3 * RWKV_WIDTH + 3 * SC_WIDTH + 2 * CF_WIDTH
PROJ_WIDTH = IN_WIDTH + 2 * LOW_WIDTH
NORM_EPS = 1e-6
LN_EPS = 1e-5
GN_EPS = 64e-5

CHUNK = 64
PAIR = 2 * HEAD_DIM
SLAB_CHUNKS = 2
ROW_TILE = 512
DOT_COLS = 256
CONV_TILE = 256
CF_HALO = 32
SC_HALO = 8
SUBLANES = 8
VMEM_LIMIT = 56 * 1024 * 1024

F32 = jnp.float32
BF16 = jnp.bfloat16


def _dot(a, b):
    return jnp.dot(a.astype(BF16), b.astype(BF16), preferred_element_type=F32)


def _dot_nt(a, b):
    return lax.dot_general(a.astype(BF16), b.astype(BF16), (((1,), (1,)), ((), ())),
                           preferred_element_type=F32)


def _dot_tn(a, b):
    return lax.dot_general(a.astype(BF16), b.astype(BF16), (((0,), (0,)), ((), ())),
                           preferred_element_type=F32)


def _rms(x, g):
    return x * lax.rsqrt(jnp.mean(x * x, axis=-1, keepdims=True) + NORM_EPS) * g


def _inproj_kernel(x_ref, g_ref, w_ref, o_ref):
    h = _rms(x_ref[...], g_ref[...]).astype(BF16)
    for j in range(PROJ_WIDTH // DOT_COLS):
        cols = slice(j * DOT_COLS, (j + 1) * DOT_COLS)
        o_ref[:, cols] = jnp.dot(h, w_ref[:, cols], preferred_element_type=F32)


def _inproj(x2, g, w):
    rows = x2.shape[0]
    return pl.pallas_call(
        _inproj_kernel,
        grid=(rows // ROW_TILE,),
        in_specs=[
            pl.BlockSpec((ROW_TILE, D_MODEL), lambda i: (i, 0)),
            pl.BlockSpec((1, D_MODEL), lambda i: (0, 0)),
            pl.BlockSpec((D_MODEL, PROJ_WIDTH), lambda i: (0, 0)),
        ],
        out_specs=pl.BlockSpec((ROW_TILE, PROJ_WIDTH), lambda i: (i, 0)),
        out_shape=jax.ShapeDtypeStruct((rows, PROJ_WIDTH), F32),
        compiler_params=pltpu.CompilerParams(
            dimension_semantics=("parallel",), vmem_limit_bytes=VMEM_LIMIT),
        name="inproj",
    )(x2, g, w)


_MU_R, _MU_K, _MU_V, _W0, _A0, _K_K, _K_A, _R_K, _LNX_G, _LNX_B = range(10)
_PVEC_ROWS = 16


def _rwkv_kernel(r_ref, k_ref, v_ref, la_ref, lb_ref, pv_ref, lw_ref, o_ref):
    seq = r_ref.shape[0]
    slab = SLAB_CHUNKS * CHUNK
    n_pairs = RWKV_WIDTH // PAIR
    pv = pv_ref[...]
    prow = lambda i: pv[i:i + 1, :]
    w2 = lw_ref[0:DECAY_RANK, :]
    a2 = lw_ref[DECAY_RANK:DECAY_RANK + ICLR_RANK, :]
    g2 = lw_ref[DECAY_RANK + ICLR_RANK:, :]

    iota = lambda shape, dim: lax.broadcasted_iota(jnp.int32, shape, dim)
    srow, scol = iota((slab, slab), 0), iota((slab, slab), 1)
    tri_incl = ((srow // CHUNK == scol // CHUNK) & (srow >= scol)).astype(BF16)
    first_row = iota((slab, 1), 0) == 0
    head0 = iota((1, PAIR), 1) < HEAD_DIM
    head_c = (head0, jnp.logical_not(head0))
    row4 = iota((4 * CHUNK, PAIR), 0) % (2 * CHUNK)
    col4 = iota((4 * CHUNK, PAIR), 1) % CHUNK
    keep4 = jnp.where(row4 < CHUNK, row4, row4 - CHUNK + 1) > col4
    plane = iota((1, PAIR), 1) < CHUNK
    eye_p = iota((PAIR, PAIR), 0) == iota((PAIR, PAIR), 1)
    same_head = (iota((PAIR, PAIR), 0) // HEAD_DIM) == (iota((PAIR, PAIR), 1) // HEAD_DIM)
    zeros_b = jnp.zeros((CHUNK, PAIR), BF16)

    def head_sum(x):
        outs = []
        for p in range(x.shape[1] // PAIR):
            xp = x[:, p * PAIR:(p + 1) * PAIR]
            s0 = jnp.sum(jnp.where(head0, xp, 0.0), axis=-1, keepdims=True)
            s1 = jnp.sum(jnp.where(head0, 0.0, xp), axis=-1, keepdims=True)
            outs.append(jnp.where(head0, s0, s1))
        return outs[0] if len(outs) == 1 else jnp.concatenate(outs, axis=1)

    def slab_body(i, states):
        t0 = pl.multiple_of(i * slab, slab)
        rows = pl.ds(t0, slab)
        prev = pl.ds(jnp.maximum(t0 - 1, 0), 1)
        has_prev = jnp.where(i > 0, 1.0, 0.0).astype(F32)

        def with_shift(ref):
            cur = ref[rows, :]
            last = ref[prev, :] * has_prev
            return cur, jnp.where(first_row, last, pltpu.roll(cur, 1, 0))

        def mixed(ref, mu):
            cur, sh = with_shift(ref)
            return cur + (sh - cur) * mu

        r = mixed(r_ref, prow(_MU_R))
        k = mixed(k_ref, prow(_MU_K))
        v = mixed(v_ref, prow(_MU_V))
        _, lb_sh = with_shift(lb_ref)
        low = la_ref[rows, :] + lb_sh
        w_pre = prow(_W0) + _dot(jnp.tanh(low[:, :DECAY_RANK]), w2)
        w_log = jnp.minimum(w_pre, 0.0) - jnp.log1p(jnp.exp(-jnp.abs(w_pre))) - 0.5
        log_decay = -jnp.exp(w_log)
        iclr = jax.nn.sigmoid(prow(_A0) + _dot(low[:, DECAY_RANK:DECAY_RANK + ICLR_RANK], a2))
        gate = _dot(jax.nn.sigmoid(low[:, DECAY_RANK + ICLR_RANK:]), g2)

        kk = k * prow(_K_K)
        kk = kk / jnp.maximum(jnp.sqrt(head_sum(kk * kk)), 1e-12)
        k = k * (1.0 + (iclr - 1.0) * prow(_K_A))
        b_vec = kk * iclr

        ld_hi = log_decay.astype(BF16)
        ld_lo = (log_decay - ld_hi.astype(F32)).astype(BF16)
        cum = (jnp.dot(tri_incl, ld_hi, preferred_element_type=F32)
               + jnp.dot(tri_incl, ld_lo, preferred_element_type=F32))
        e_neg = jnp.exp(-cum)
        a_t = -kk * jnp.exp(cum - log_decay)
        r_t = r * jnp.exp(cum)
        b_t = b_vec * e_neg
        k_t = k * e_neg
        bonus = head_sum(r * k * prow(_R_K)) * v

        units = [(j, p) for j in range(SLAB_CHUNKS) for p in range(n_pairs)]
        heads = range(2)
        blk = lambda x, u: x[u[0] * CHUNK:(u[0] + 1) * CHUNK, u[1] * PAIR:(u[1] + 1) * PAIR]
        cum_end = {u: cum[(u[0] + 1) * CHUNK - 1:(u[0] + 1) * CHUNK, u[1] * PAIR:(u[1] + 1) * PAIR]
                   for u in units}
        at = {u: [jnp.where(head_c[h], blk(a_t, u), 0.0) for h in heads] for u in units}
        rtb = {u: blk(r_t, u).astype(BF16) for u in units}
        vb = {u: blk(v, u).astype(BF16) for u in units}
        rt = {u: [jnp.where(head_c[h], rtb[u], zeros_b) for h in heads] for u in units}
        vh = {u: [jnp.where(head_c[h], vb[u], zeros_b) for h in heads] for u in units}
        xab = {u: [at[u][h].astype(BF16) for h in heads] for u in units}
        big = {u: jnp.where(keep4,
                            _dot_nt(jnp.concatenate([xab[u][0], rt[u][0], xab[u][1], rt[u][1]], axis=0),
                                    jnp.concatenate([blk(b_t, u), blk(k_t, u)], axis=0)), 0.0)
               for u in units}
        bot = {u: [big[u][(2 * h + 1) * CHUNK:(2 * h + 2) * CHUNK] for h in heads] for u in units}
        px = {u: [big[u][2 * h * CHUNK:(2 * h + 1) * CHUNK] for h in heads] for u in units}
        xa = at
        pxb = {u: [px[u][h].astype(BF16) for h in heads] for u in units}
        for _ in range(6):
            res = {u: [_dot(pxb[u][h][:, :CHUNK], jnp.concatenate([pxb[u][h], xab[u][h]], axis=1))
                       for h in heads] for u in units}
            px = {u: [res[u][h][:, :PAIR] + jnp.where(plane, 0.0, px[u][h]) for h in heads] for u in units}
            xa = {u: [res[u][h][:, PAIR:] + xa[u][h] for h in heads] for u in units}
            pxb = {u: [px[u][h].astype(BF16) for h in heads] for u in units}
            xab = {u: [xa[u][h].astype(BF16) for h in heads] for u in units}
        u_v = {u: [_dot(pxb[u][h], jnp.concatenate([zeros_b, vh[u][h]], axis=0)) for h in heads]
               for u in units}
        ry = {u: [_dot(bot[u][h],
                       jnp.concatenate([jnp.concatenate([xab[u][h], u_v[u][h].astype(BF16)], axis=1),
                                        jnp.concatenate([zeros_b, vh[u][h]], axis=1)], axis=0))
                  for h in heads] for u in units}
        mn = {}
        for u in units:
            e_end = jnp.exp(cum_end[u] - blk(cum, u))
            bk_h = jnp.concatenate([blk(b_vec, u) * e_end, blk(k, u) * e_end], axis=0)
            apuv = jnp.concatenate(
                [jnp.concatenate([xa[u][0] + xa[u][1], u_v[u][0] + u_v[u][1]], axis=1).astype(BF16),
                 jnp.concatenate([zeros_b, vb[u]], axis=1)], axis=0)
            mn[u] = _dot_tn(bk_h, apuv)
        states = list(states)
        for u in units:
            j, p = u
            m = mn[u][:, :PAIR] + jnp.where(eye_p, jnp.exp(cum_end[u]), 0.0)
            r_p = blk(r_t, u) + ry[u][0][:, :PAIR] + ry[u][1][:, :PAIR]
            out = _dot(jnp.concatenate([m, r_p], axis=0), states[p])
            states[p] = jnp.where(same_head, out[:PAIR] + mn[u][:, PAIR:], 0.0)
            y = out[PAIR:] + ry[u][0][:, PAIR:] + ry[u][1][:, PAIR:]
            yc = y - head_sum(y) * (1.0 / HEAD_DIM)
            var = head_sum(yc * yc) * (1.0 / HEAD_DIM)
            lanes = slice(p * PAIR, (p + 1) * PAIR)
            y = yc * lax.rsqrt(var + GN_EPS) * prow(_LNX_G)[:, lanes] + prow(_LNX_B)[:, lanes]
            o_ref[pl.ds(t0 + j * CHUNK, CHUNK), lanes] = ((y + blk(bonus, u)) * blk(gate, u)).astype(o_ref.dtype)
        return tuple(states)

    zero_state = jnp.zeros((PAIR, PAIR), F32)
    lax.fori_loop(0, seq // slab, slab_body, (zero_state,) * n_pairs)


def _rwkv(proj3, pvec, lowrank_w):
    bsz, seq, _ = proj3.shape
    col = lambda blk: pl.BlockSpec((None, seq, RWKV_WIDTH), lambda b, blk=blk: (b, 0, blk))
    low = lambda blk: pl.BlockSpec((None, seq, LOW_WIDTH), lambda b, blk=blk: (b, 0, blk))
    full = lambda a: pl.BlockSpec(a.shape, lambda b: (0, 0))
    return pl.pallas_call(
        _rwkv_kernel,
        grid=(bsz,),
        in_specs=[col(0), col(1), col(2),
                  low(IN_WIDTH // LOW_WIDTH), low(IN_WIDTH // LOW_WIDTH + 1),
                  full(pvec), full(lowrank_w)],
        out_specs=pl.BlockSpec((None, seq, RWKV_WIDTH), lambda b: (b, 0, 0)),
        out_shape=jax.ShapeDtypeStruct((bsz, seq, RWKV_WIDTH), BF16),
        compiler_params=pltpu.CompilerParams(
            dimension_semantics=("parallel",), vmem_limit_bytes=VMEM_LIMIT),
        name="rwkv",
    )(proj3, proj3, proj3, proj3, proj3, pvec, lowrank_w)


def _conv_kernel(gb_ref, gc_ref, u_ref, val_ref, cg_ref, scw_ref, cfw_ref, cfp_ref, o_ref,
                 sc_buf, cf_buf, cf_sh):
    tile = gb_ref.shape[0]

    @pl.when(pl.program_id(1) == 0)
    def _():
        sc_buf[0:SC_HALO, :] = jnp.zeros((SC_HALO, SC_WIDTH), F32)
        cf_buf[0:CF_HALO, :] = jnp.zeros((CF_HALO, CF_WIDTH), F32)

    sc_buf[SC_HALO:, :] = gc_ref[...] * u_ref[...]
    acc = jnp.zeros((tile, SC_WIDTH), F32)
    for j in range(SC_KERNEL):
        acc = acc + scw_ref[j:j + 1, :] * sc_buf[pl.ds(SC_HALO - (SC_KERNEL - 1) + j, tile), :]
    o_ref[:, 0:SC_WIDTH] = (gb_ref[...] * acc).astype(o_ref.dtype)

    cf_buf[CF_HALO:, :] = val_ref[...] * jax.nn.sigmoid(cg_ref[...])
    n_sh = cf_sh.shape[1]
    for res in range(1, SUBLANES):
        cf_sh[res - 1] = cf_buf[pl.ds(res, n_sh), :]
    acc = jnp.zeros((tile, CF_WIDTH), F32) + cfp_ref[0:1, :]
    for j in range(CF_KERNEL):
        start = CF_HALO - (CF_KERNEL - 1) + j
        res = start % SUBLANES
        if res == 0:
            taps = cf_buf[pl.ds(start, tile), :]
        else:
            taps = cf_sh[res - 1, pl.ds(start - res, tile), :]
        acc = acc + cfw_ref[j:j + 1, :] * taps
    mu = jnp.mean(acc, axis=-1, keepdims=True)
    cen = acc - mu
    var = jnp.mean(cen * cen, axis=-1, keepdims=True)
    z = cen * lax.rsqrt(var + LN_EPS) * cfp_ref[1:2, :] + cfp_ref[2:3, :]
    o_ref[:, SC_WIDTH:] = (z * jax.nn.sigmoid(z)).astype(o_ref.dtype)

    sc_buf[0:SC_HALO, :] = sc_buf[tile:tile + SC_HALO, :]
    cf_buf[0:CF_HALO, :] = cf_buf[tile:tile + CF_HALO, :]


def _conv(proj3, sc_w, cf_w, cf_p):
    bsz, seq, _ = proj3.shape
    base = 3 * RWKV_WIDTH // SC_WIDTH
    col = lambda blk: pl.BlockSpec((None, CONV_TILE, SC_WIDTH), lambda b, t, blk=blk: (b, t, blk))
    full = lambda a: pl.BlockSpec(a.shape, lambda b, t: (0, 0))
    return pl.pallas_call(
        _conv_kernel,
        grid=(bsz, seq // CONV_TILE),
        in_specs=[col(base), col(base + 1), col(base + 2), col(base + 3), col(base + 4),
                  full(sc_w), full(cf_w), full(cf_p)],
        out_specs=pl.BlockSpec((None, CONV_TILE, SC_WIDTH + CF_WIDTH), lambda b, t: (b, t, 0)),
        out_shape=jax.ShapeDtypeStruct((bsz, seq, SC_WIDTH + CF_WIDTH), BF16),
        scratch_shapes=[pltpu.VMEM((SC_HALO + CONV_TILE, SC_WIDTH), F32),
                        pltpu.VMEM((CF_HALO + CONV_TILE, CF_WIDTH), F32),
                        pltpu.VMEM((SUBLANES - 1, CF_HALO + CONV_TILE - SUBLANES, CF_WIDTH), F32)],
        compiler_params=pltpu.CompilerParams(
            dimension_semantics=("parallel", "arbitrary"), vmem_limit_bytes=VMEM_LIMIT),
        name="conv",
    )(proj3, proj3, proj3, proj3, proj3, sc_w, cf_w, cf_p)


def _outffn_kernel(x_ref, yr_ref, yc_ref, wo_ref, wg_ref, wu_ref, wd_ref, g_ref, o_ref):
    mix = (jnp.dot(yr_ref[...], wo_ref[0:RWKV_WIDTH, :], preferred_element_type=F32)
           + jnp.dot(yc_ref[...], wo_ref[RWKV_WIDTH:, :], preferred_element_type=F32))
    x = x_ref[...] + _rms(mix, g_ref[0:1, :])
    h2 = _rms(x, g_ref[1:2, :]).astype(BF16)
    gate = jnp.dot(h2, wg_ref[...], preferred_element_type=F32)
    up = jnp.dot(h2, wu_ref[...], preferred_element_type=F32)
    act = (gate * jax.nn.sigmoid(gate) * up).astype(BF16)
    f = jnp.dot(act, wd_ref[...], preferred_element_type=F32)
    o_ref[...] = x + _rms(f, g_ref[2:3, :])


def _outffn(x2, y_rwkv, y_conv, wo, wg, wu, wd, gains):
    rows = x2.shape[0]
    tile = lambda a: pl.BlockSpec((ROW_TILE, a.shape[1]), lambda i: (i, 0))
    full = lambda a: pl.BlockSpec(a.shape, lambda i: (0, 0), pipeline_mode=pl.Buffered(1))
    return pl.pallas_call(
        _outffn_kernel,
        grid=(rows // ROW_TILE,),
        in_specs=[tile(x2), tile(y_rwkv), tile(y_conv),
                  full(wo), full(wg), full(wu), full(wd), full(gains)],
        out_specs=tile(x2),
        out_shape=jax.ShapeDtypeStruct(x2.shape, F32),
        compiler_params=pltpu.CompilerParams(
            dimension_semantics=("parallel",), vmem_limit_bytes=VMEM_LIMIT),
        name="outffn",
    )(x2, y_rwkv, y_conv, wo, wg, wu, wd, gains)


def _fold_low_rank(mu, w):
    return (1.0 - mu)[:, None] * w, mu[:, None] * w


def kernel(x, w_in, mu_rkv, mu_wag, w0, w1, w2, a0, a1, a2, g1, g2, k_k, k_a, r_k, lnx_g, lnx_b,
           sc_conv_w, cf_conv_w, cf_conv_b, cf_ln_g, cf_ln_b, w_o, w_gate, w_up, w_down,
           pre_mix_g, post_mix_g, pre_ffn_g, post_ffn_g):
    bsz, seq, d = x.shape
    depth = w_in.shape[0]
    x2 = x.reshape(bsz * seq, d)
    for l in range(depth):
        w1a, w1b = _fold_low_rank(mu_wag[l, 0], w1[l])
        a1a, a1b = _fold_low_rank(mu_wag[l, 1], a1[l])
        g1a, g1b = _fold_low_rank(mu_wag[l, 2], g1[l])
        w_cat = jnp.concatenate([w_in[l], w1a, a1a, g1a, w1b, a1b, g1b], axis=1).astype(BF16)
        pvec = jnp.concatenate([
            mu_rkv[l].reshape(3, RWKV_WIDTH),
            jnp.stack([w0[l], a0[l], k_k[l], k_a[l], r_k[l].reshape(-1), lnx_g[l], lnx_b[l]]),
            jnp.zeros((_PVEC_ROWS - 10, RWKV_WIDTH), F32)], axis=0)
        lowrank_w = jnp.concatenate([w2[l], a2[l], g2[l]], axis=0).astype(BF16)
        cf_p = jnp.stack([cf_conv_b[l], cf_ln_g[l], cf_ln_b[l]])
        gains = jnp.stack([post_mix_g[l], pre_ffn_g[l], post_ffn_g[l]])

        proj = _inproj(x2, pre_mix_g[l][None, :], w_cat)
        proj3 = proj.reshape(bsz, seq, PROJ_WIDTH)
        y_rwkv = _rwkv(proj3, pvec, lowrank_w).reshape(bsz * seq, RWKV_WIDTH)
        y_conv = _conv(proj3, sc_conv_w[l], cf_conv_w[l], cf_p).reshape(bsz * seq, SC_WIDTH + CF_WIDTH)
        x2 = _outffn(x2, y_rwkv, y_conv, w_o[l].astype(BF16), w_gate[l].astype(BF16),
                     w_up[l].astype(BF16), w_down[l].astype(BF16), gains)
    return x2.reshape(bsz, seq, d)
```

```python
import functools

import jax
import jax.numpy as jnp
from jax import lax
from jax.experimental import pallas as pl
from jax.experimental.pallas import tpu as pltpu

D_MODEL = 1024
HEAD_DIM = 64
RWKV_WIDTH = 512
SC_WIDTH = 256
CF_WIDTH = 256
SC_KERNEL = 3
CF_KERNEL = 31
DECAY_RANK = 64
ICLR_RANK = 64
GATE_RANK = 128
LOW_WIDTH = DECAY_RANK + ICLR_RANK + GATE_RANK
IN_WIDTH = 3 * RWKV_WIDTH + 3 * SC_WIDTH + 2 * CF_WIDTH
PROJ_WIDTH = IN_WIDTH + 2 * LOW_WIDTH
NORM_EPS = 1e-6
LN_EPS = 1e-5
GN_EPS = 64e-5

CHUNK = 64
PAIR = 2 * HEAD_DIM
SLAB_CHUNKS = 2
ROW_TILE = 512
DOT_COLS = 256
CONV_COLS = 3 * SC_WIDTH + 2 * CF_WIDTH
RWKV_COLS = PROJ_WIDTH - CONV_COLS
CONV_ROWS = 32
CONV_BLOCKS_PER_DOT = 2
OUT_ROWS = 128
CF_HALO = 32
SC_HALO = 8
SUBLANES = 8
VMEM_LIMIT = 56 * 1024 * 1024

F32 = jnp.float32
BF16 = jnp.bfloat16


def _dot(a, b):
    return jnp.dot(a.astype(BF16), b.astype(BF16), preferred_element_type=F32)


def _dot_nt(a, b):
    return lax.dot_general(a.astype(BF16), b.astype(BF16), (((1,), (1,)), ((), ())),
                           preferred_element_type=F32)


def _dot_tn(a, b):
    return lax.dot_general(a.astype(BF16), b.astype(BF16), (((0,), (0,)), ((), ())),
                           preferred_element_type=F32)


def _rms(x, g):
    return x * lax.rsqrt(jnp.mean(x * x, axis=-1, keepdims=True) + NORM_EPS) * g


def _inproj_conv_kernel(x_ref, g_ref, w_ref, woc_ref, scw_ref, cfw_ref, cfp_ref, proj_ref, mixc_ref,
                        held, gb_buf, sc_buf, cf_buf, cf_sh, y_buf, *, tiles_per_seq):
    i = pl.program_id(0)
    tile = x_ref.shape[0]

    @pl.when(i == 0)
    def _():
        held[...] = jnp.zeros(held.shape, F32)
        sc_buf[0:SC_HALO, :] = jnp.zeros((SC_HALO, SC_WIDTH), F32)
        cf_buf[0:CF_HALO, :] = jnp.zeros((CF_HALO, CF_WIDTH), F32)

    col = lambda n: slice(n * SC_WIDTH, (n + 1) * SC_WIDTH)
    gb_buf[...] = held[:, col(0)]
    sc_buf[SC_HALO:, :] = held[:, col(1)] * held[:, col(2)]
    cf_buf[CF_HALO:, :] = held[:, col(3)] * jax.nn.sigmoid(held[:, col(4)])

    n_sh = cf_sh.shape[1]
    for res in range(1, SUBLANES):
        cf_sh[res - 1] = cf_buf[pl.ds(res, n_sh), :]

    h = _rms(x_ref[...], g_ref[...]).astype(BF16)

    def project(j):
        cols = slice(j * DOT_COLS, (j + 1) * DOT_COLS)
        out = jnp.dot(h, w_ref[:, cols], preferred_element_type=F32)
        if j < RWKV_COLS // DOT_COLS:
            proj_ref[:, cols] = out
        else:
            held[:, j * DOT_COLS - RWKV_COLS:(j + 1) * DOT_COLS - RWKV_COLS] = out

    def conv_rows(r0):
        acc = jnp.zeros((CONV_ROWS, SC_WIDTH), F32)
        for j in range(SC_KERNEL):
            acc = acc + scw_ref[j:j + 1, :] * sc_buf[pl.ds(r0 + SC_HALO - (SC_KERNEL - 1) + j, CONV_ROWS), :]
        y_buf[r0:r0 + CONV_ROWS, 0:SC_WIDTH] = (gb_buf[r0:r0 + CONV_ROWS, :] * acc).astype(y_buf.dtype)
        acc = jnp.zeros((CONV_ROWS, CF_WIDTH), F32) + cfp_ref[0:1, :]
        for j in range(CF_KERNEL):
            start = r0 + CF_HALO - (CF_KERNEL - 1) + j
            res = start % SUBLANES
            if res == 0:
                taps = cf_buf[pl.ds(start, CONV_ROWS), :]
            else:
                taps = cf_sh[res - 1, pl.ds(start - res, CONV_ROWS), :]
            acc = acc + cfw_ref[j:j + 1, :] * taps
        mu = jnp.mean(acc, axis=-1, keepdims=True)
        cen = acc - mu
        var = jnp.mean(cen * cen, axis=-1, keepdims=True)
        z = cen * lax.rsqrt(var + LN_EPS) * cfp_ref[1:2, :] + cfp_ref[2:3, :]
        y_buf[r0:r0 + CONV_ROWS, SC_WIDTH:] = (z * jax.nn.sigmoid(z)).astype(y_buf.dtype)

    n_conv = tile // CONV_ROWS
    done = 0
    for j in range(PROJ_WIDTH // DOT_COLS):
        project(j)
        upto = min(n_conv, done + CONV_BLOCKS_PER_DOT)
        for c in range(done, upto):
            conv_rows(c * CONV_ROWS)
            r1 = (c + 1) * CONV_ROWS
            if r1 % OUT_ROWS == 0:
                rows = slice(r1 - OUT_ROWS, r1)
                mixc_ref[rows, :] = jnp.dot(y_buf[rows, :], woc_ref[...], preferred_element_type=F32)
        done = upto
    assert done == n_conv

    keep = jnp.where(i % tiles_per_seq != 0, 1.0, 0.0).astype(F32)
    sc_buf[0:SC_HALO, :] = sc_buf[tile:tile + SC_HALO, :] * keep
    cf_buf[0:CF_HALO, :] = cf_buf[tile:tile + CF_HALO, :] * keep


def _inproj_conv(x2, g, w, wo_conv, sc_w, cf_w, cf_p, seq):
    rows = x2.shape[0]
    n_tiles = rows // ROW_TILE
    full = lambda a: pl.BlockSpec(a.shape, lambda i: (0, 0))
    cur = lambda i: (jnp.minimum(i, n_tiles - 1), 0)
    return pl.pallas_call(
        functools.partial(_inproj_conv_kernel, tiles_per_seq=seq // ROW_TILE),
        grid=(n_tiles + 1,),
        in_specs=[
            pl.BlockSpec((ROW_TILE, D_MODEL), cur),
            full(g),
            pl.BlockSpec(w.shape, lambda i: (0, 0), pipeline_mode=pl.Buffered(1)),
            full(wo_conv), full(sc_w), full(cf_w), full(cf_p),
        ],
        out_specs=[pl.BlockSpec((ROW_TILE, RWKV_COLS), cur),
                   pl.BlockSpec((ROW_TILE, D_MODEL), lambda i: (jnp.maximum(i - 1, 0), 0))],
        out_shape=[jax.ShapeDtypeStruct((rows, RWKV_COLS), F32),
                   jax.ShapeDtypeStruct((rows, D_MODEL), F32)],
        scratch_shapes=[pltpu.VMEM((ROW_TILE, CONV_COLS), F32),
                        pltpu.VMEM((ROW_TILE, SC_WIDTH), F32),
                        pltpu.VMEM((SC_HALO + ROW_TILE, SC_WIDTH), F32),
                        pltpu.VMEM((CF_HALO + ROW_TILE, CF_WIDTH), F32),
                        pltpu.VMEM((SUBLANES - 1, CF_HALO + ROW_TILE - SUBLANES, CF_WIDTH), F32),
                        pltpu.VMEM((ROW_TILE, SC_WIDTH + CF_WIDTH), BF16)],
        compiler_params=pltpu.CompilerParams(
            dimension_semantics=("arbitrary",), vmem_limit_bytes=VMEM_LIMIT),
        name="inproj_conv",
    )(x2, g, w, wo_conv, sc_w, cf_w, cf_p)


_MU_R, _MU_K, _MU_V, _W0, _A0, _K_K, _K_A, _R_K, _LNX_G, _LNX_B = range(10)
_PVEC_ROWS = 16


def _rwkv_kernel(r_ref, k_ref, v_ref, la_ref, lb_ref, pv_ref, lw_ref, o_ref):
    seq = r_ref.shape[0]
    slab = SLAB_CHUNKS * CHUNK
    n_pairs = RWKV_WIDTH // PAIR
    pv = pv_ref[...]
    prow = lambda i: pv[i:i + 1, :]
    w2 = lw_ref[0:DECAY_RANK, :]
    a2 = lw_ref[DECAY_RANK:DECAY_RANK + ICLR_RANK, :]
    g2 = lw_ref[DECAY_RANK + ICLR_RANK:, :]

    iota = lambda shape, dim: lax.broadcasted_iota(jnp.int32, shape, dim)
    srow, scol = iota((slab, slab), 0), iota((slab, slab), 1)
    tri_incl = ((srow // CHUNK == scol // CHUNK) & (srow >= scol)).astype(BF16)
    first_row = iota((slab, 1), 0) == 0
    head0 = iota((1, PAIR), 1) < HEAD_DIM
    head_c = (head0, jnp.logical_not(head0))
    row4 = iota((4 * CHUNK, PAIR), 0) % (2 * CHUNK)
    col4 = iota((4 * CHUNK, PAIR), 1) % CHUNK
    keep4 = jnp.where(row4 < CHUNK, row4, row4 - CHUNK + 1) > col4
    plane = iota((1, PAIR), 1) < CHUNK
    eye_p = iota((PAIR, PAIR), 0) == iota((PAIR, PAIR), 1)
    same_head = (iota((PAIR, PAIR), 0) // HEAD_DIM) == (iota((PAIR, PAIR), 1) // HEAD_DIM)
    zeros_b = jnp.zeros((CHUNK, PAIR), BF16)

    def head_sum(x):
        outs = []
        for p in range(x.shape[1] // PAIR):
            xp = x[:, p * PAIR:(p + 1) * PAIR]
            s0 = jnp.sum(jnp.where(head0, xp, 0.0), axis=-1, keepdims=True)
            s1 = jnp.sum(jnp.where(head0, 0.0, xp), axis=-1, keepdims=True)
            outs.append(jnp.where(head0, s0, s1))
        return outs[0] if len(outs) == 1 else jnp.concatenate(outs, axis=1)

    def slab_body(i, states):
        t0 = pl.multiple_of(i * slab, slab)
        rows = pl.ds(t0, slab)
        prev = pl.ds(jnp.maximum(t0 - 1, 0), 1)
        has_prev = jnp.where(i > 0, 1.0, 0.0).astype(F32)

        def with_shift(ref):
            cur = ref[rows, :]
            last = ref[prev, :] * has_prev
            return cur, jnp.where(first_row, last, pltpu.roll(cur, 1, 0))

        def mixed(ref, mu):
            cur, sh = with_shift(ref)
            return cur + (sh - cur) * mu

        r = mixed(r_ref, prow(_MU_R))
        k = mixed(k_ref, prow(_MU_K))
        v = mixed(v_ref, prow(_MU_V))
        _, lb_sh = with_shift(lb_ref)
        low = la_ref[rows, :] + lb_sh
        w_pre = prow(_W0) + _dot(jnp.tanh(low[:, :DECAY_RANK]), w2)
        w_log = jnp.minimum(w_pre, 0.0) - jnp.log1p(jnp.exp(-jnp.abs(w_pre))) - 0.5
        log_decay = -jnp.exp(w_log)
        iclr = jax.nn.sigmoid(prow(_A0) + _dot(low[:, DECAY_RANK:DECAY_RANK + ICLR_RANK], a2))
        gate = _dot(jax.nn.sigmoid(low[:, DECAY_RANK + ICLR_RANK:]), g2)

        kk = k * prow(_K_K)
        kk = kk / jnp.maximum(jnp.sqrt(head_sum(kk * kk)), 1e-12)
        k = k * (1.0 + (iclr - 1.0) * prow(_K_A))
        b_vec = kk * iclr

        ld_hi = log_decay.astype(BF16)
        ld_lo = (log_decay - ld_hi.astype(F32)).astype(BF16)
        cum = (jnp.dot(tri_incl, ld_hi, preferred_element_type=F32)
               + jnp.dot(tri_incl, ld_lo, preferred_element_type=F32))
        e_neg = jnp.exp(-cum)
        a_t = -kk * jnp.exp(cum - log_decay)
        r_t = r * jnp.exp(cum)
        b_t = b_vec * e_neg
        k_t = k * e_neg
        bonus = head_sum(r * k * prow(_R_K)) * v

        units = [(j, p) for j in range(SLAB_CHUNKS) for p in range(n_pairs)]
        heads = range(2)
        blk = lambda x, u: x[u[0] * CHUNK:(u[0] + 1) * CHUNK, u[1] * PAIR:(u[1] + 1) * PAIR]
        cum_end = {u: cum[(u[0] + 1) * CHUNK - 1:(u[0] + 1) * CHUNK, u[1] * PAIR:(u[1] + 1) * PAIR]
                   for u in units}
        at = {u: [jnp.where(head_c[h], blk(a_t, u), 0.0) for h in heads] for u in units}
        rtb = {u: blk(r_t, u).astype(BF16) for u in units}
        vb = {u: blk(v, u).astype(BF16) for u in units}
        rt = {u: [jnp.where(head_c[h], rtb[u], zeros_b) for h in heads] for u in units}
        vh = {u: [jnp.where(head_c[h], vb[u], zeros_b) for h in heads] for u in units}
        xab = {u: [at[u][h].astype(BF16) for h in heads] for u in units}
        big = {u: jnp.where(keep4,
                            _dot_nt(jnp.concatenate([xab[u][0], rt[u][0], xab[u][1], rt[u][1]], axis=0),
                                    jnp.concatenate([blk(b_t, u), blk(k_t, u)], axis=0)), 0.0)
               for u in units}
        bot = {u: [big[u][(2 * h + 1) * CHUNK:(2 * h + 2) * CHUNK] for h in heads] for u in units}
        px = {u: [big[u][2 * h * CHUNK:(2 * h + 1) * CHUNK] for h in heads] for u in units}
        xa = at
        pxb = {u: [px[u][h].astype(BF16) for h in heads] for u in units}
        for _ in range(6):
            res = {u: [_dot(pxb[u][h][:, :CHUNK], jnp.concatenate([pxb[u][h], xab[u][h]], axis=1))
                       for h in heads] for u in units}
            px = {u: [res[u][h][:, :PAIR] + jnp.where(plane, 0.0, px[u][h]) for h in heads] for u in units}
            xa = {u: [res[u][h][:, PAIR:] + xa[u][h] for h in heads] for u in units}
            pxb = {u: [px[u][h].astype(BF16) for h in heads] for u in units}
            xab = {u: [xa[u][h].astype(BF16) for h in heads] for u in units}
        u_v = {u: [_dot(pxb[u][h], jnp.concatenate([zeros_b, vh[u][h]], axis=0)) for h in heads]
               for u in units}
        ry = {u: [_dot(bot[u][h],
                       jnp.concatenate([jnp.concatenate([xab[u][h], u_v[u][h].astype(BF16)], axis=1),
                                        jnp.concatenate([zeros_b, vh[u][h]], axis=1)], axis=0))
                  for h in heads] for u in units}
        mn = {}
        for u in units:
            e_end = jnp.exp(cum_end[u] - blk(cum, u))
            bk_h = jnp.concatenate([blk(b_vec, u) * e_end, blk(k, u) * e_end], axis=0)
            apuv = jnp.concatenate(
                [jnp.concatenate([xa[u][0] + xa[u][1], u_v[u][0] + u_v[u][1]], axis=1).astype(BF16),
                 jnp.concatenate([zeros_b, vb[u]], axis=1)], axis=0)
            mn[u] = _dot_tn(bk_h, apuv)
        states = list(states)
        for u in units:
            j, p = u
            m = mn[u][:, :PAIR] + jnp.where(eye_p, jnp.exp(cum_end[u]), 0.0)
            r_p = blk(r_t, u) + ry[u][0][:, :PAIR] + ry[u][1][:, :PAIR]
            out = _dot(jnp.concatenate([m, r_p], axis=0), states[p])
            states[p] = jnp.where(same_head, out[:PAIR] + mn[u][:, PAIR:], 0.0)
            y = out[PAIR:] + ry[u][0][:, PAIR:] + ry[u][1][:, PAIR:]
            yc = y - head_sum(y) * (1.0 / HEAD_DIM)
            var = head_sum(yc * yc) * (1.0 / HEAD_DIM)
            lanes = slice(p * PAIR, (p + 1) * PAIR)
            y = yc * lax.rsqrt(var + GN_EPS) * prow(_LNX_G)[:, lanes] + prow(_LNX_B)[:, lanes]
            o_ref[pl.ds(t0 + j * CHUNK, CHUNK), lanes] = ((y + blk(bonus, u)) * blk(gate, u)).astype(o_ref.dtype)
        return tuple(states)

    zero_state = jnp.zeros((PAIR, PAIR), F32)
    lax.fori_loop(0, seq // slab, slab_body, (zero_state,) * n_pairs)


def _rwkv(proj3, pvec, lowrank_w):
    bsz, seq, _ = proj3.shape
    col = lambda blk: pl.BlockSpec((None, seq, RWKV_WIDTH), lambda b, blk=blk: (b, 0, blk))
    low = lambda blk: pl.BlockSpec((None, seq, LOW_WIDTH), lambda b, blk=blk: (b, 0, blk))
    full = lambda a: pl.BlockSpec(a.shape, lambda b: (0, 0))
    return pl.pallas_call(
        _rwkv_kernel,
        grid=(bsz,),
        in_specs=[col(0), col(1), col(2),
                  low(3 * RWKV_WIDTH // LOW_WIDTH), low(3 * RWKV_WIDTH // LOW_WIDTH + 1),
                  full(pvec), full(lowrank_w)],
        out_specs=pl.BlockSpec((None, seq, RWKV_WIDTH), lambda b: (b, 0, 0)),
        out_shape=jax.ShapeDtypeStruct((bsz, seq, RWKV_WIDTH), BF16),
        compiler_params=pltpu.CompilerParams(
            dimension_semantics=("parallel",), vmem_limit_bytes=VMEM_LIMIT),
        name="rwkv",
    )(proj3, proj3, proj3, proj3, proj3, pvec, lowrank_w)


def _outffn_kernel(x_ref, yr_ref, mixc_ref, wo_ref, wg_ref, wu_ref, wd_ref, g_ref, o_ref):
    mix = jnp.dot(yr_ref[...], wo_ref[...], preferred_element_type=F32) + mixc_ref[...]
    x = x_ref[...] + _rms(mix, g_ref[0:1, :])
    h2 = _rms(x, g_ref[1:2, :]).astype(BF16)
    gate = jnp.dot(h2, wg_ref[...], preferred_element_type=F32)
    up = jnp.dot(h2, wu_ref[...], preferred_element_type=F32)
    act = (gate * jax.nn.sigmoid(gate) * up).astype(BF16)
    f = jnp.dot(act, wd_ref[...], preferred_element_type=F32)
    o_ref[...] = x + _rms(f, g_ref[2:3, :])


def _outffn(x2, y_rwkv, mix_conv, wo, wg, wu, wd, gains):
    rows = x2.shape[0]
    tile = lambda a: pl.BlockSpec((ROW_TILE, a.shape[1]), lambda i: (i, 0))
    full = lambda a: pl.BlockSpec(a.shape, lambda i: (0, 0), pipeline_mode=pl.Buffered(1))
    return pl.pallas_call(
        _outffn_kernel,
        grid=(rows // ROW_TILE,),
        in_specs=[tile(x2), tile(y_rwkv), tile(mix_conv),
                  full(wo), full(wg), full(wu), full(wd), full(gains)],
        out_specs=tile(x2),
        out_shape=jax.ShapeDtypeStruct(x2.shape, F32),
        compiler_params=pltpu.CompilerParams(
            dimension_semantics=("parallel",), vmem_limit_bytes=VMEM_LIMIT),
        name="outffn",
    )(x2, y_rwkv, mix_conv, wo, wg, wu, wd, gains)


def _fold_low_rank(mu, w):
    return (1.0 - mu)[:, None] * w, mu[:, None] * w


def kernel(x, w_in, mu_rkv, mu_wag, w0, w1, w2, a0, a1, a2, g1, g2, k_k, k_a, r_k, lnx_g, lnx_b,
           sc_conv_w, cf_conv_w, cf_conv_b, cf_ln_g, cf_ln_b, w_o, w_gate, w_up, w_down,
           pre_mix_g, post_mix_g, pre_ffn_g, post_ffn_g):
    bsz, seq, d = x.shape
    depth = w_in.shape[0]
    x2 = x.reshape(bsz * seq, d)
    for l in range(depth):
        w1a, w1b = _fold_low_rank(mu_wag[l, 0], w1[l])
        a1a, a1b = _fold_low_rank(mu_wag[l, 1], a1[l])
        g1a, g1b = _fold_low_rank(mu_wag[l, 2], g1[l])
        w_cat = jnp.concatenate([w_in[l][:, :3 * RWKV_WIDTH], w1a, a1a, g1a, w1b, a1b, g1b,
                                 w_in[l][:, 3 * RWKV_WIDTH:]], axis=1).astype(BF16)
        pvec = jnp.concatenate([
            mu_rkv[l].reshape(3, RWKV_WIDTH),
            jnp.stack([w0[l], a0[l], k_k[l], k_a[l], r_k[l].reshape(-1), lnx_g[l], lnx_b[l]]),
            jnp.zeros((_PVEC_ROWS - 10, RWKV_WIDTH), F32)], axis=0)
        lowrank_w = jnp.concatenate([w2[l], a2[l], g2[l]], axis=0).astype(BF16)
        cf_p = jnp.stack([cf_conv_b[l], cf_ln_g[l], cf_ln_b[l]])
        gains = jnp.stack([post_mix_g[l], pre_ffn_g[l], post_ffn_g[l]])

        wo = w_o[l].astype(BF16)
        proj, mix_conv = _inproj_conv(x2, pre_mix_g[l][None, :], w_cat, wo[RWKV_WIDTH:],
                                      sc_conv_w[l], cf_conv_w[l], cf_p, seq)
        proj3 = proj.reshape(bsz, seq, RWKV_COLS)
        y_rwkv = _rwkv(proj3, pvec, lowrank_w).reshape(bsz * seq, RWKV_WIDTH)
        x2 = _outffn(x2, y_rwkv, mix_conv, wo[:RWKV_WIDTH], w_gate[l].astype(BF16),
                     w_up[l].astype(BF16), w_down[l].astype(BF16), gains)
    return x2.reshape(bsz, seq, d)
```

```python
import functools
import math

import jax
import jax.numpy as jnp
from jax import lax
from jax.experimental import pallas as pl
from jax.experimental.pallas import tpu as pltpu

D_MODEL = 1024
HEAD_DIM = 64
RWKV_WIDTH = 512
SC_WIDTH = 256
CF_WIDTH = 256
SC_KERNEL = 3
CF_KERNEL = 31
DECAY_RANK = 64
ICLR_RANK = 64
GATE_RANK = 128
LOW_WIDTH = DECAY_RANK + ICLR_RANK + GATE_RANK
IN_WIDTH = 3 * RWKV_WIDTH + 3 * SC_WIDTH + 2 * CF_WIDTH
PROJ_WIDTH = IN_WIDTH + 2 * LOW_WIDTH
NORM_EPS = 1e-6
LN_EPS = 1e-5
GN_EPS = 64e-5
MIX_WIDTH = RWKV_WIDTH + SC_WIDTH + CF_WIDTH
DECAY_SCALE = math.exp(-0.5)

CHUNK = 64
PAIR = 2 * HEAD_DIM
SLAB_CHUNKS = 2
ROW_TILE = 512
DOT_COLS = 256
CONV_COLS = 3 * SC_WIDTH + 2 * CF_WIDTH
RWKV_COLS = PROJ_WIDTH - CONV_COLS
CONV_ROWS = 32
CONV_BLOCKS_PER_DOT = 2
OUT_ROWS = 128
CF_HALO = 32
SC_HALO = 8
SUBLANES = 8
VMEM_LIMIT = 56 * 1024 * 1024

F32 = jnp.float32
BF16 = jnp.bfloat16


def _dot(a, b):
    return jnp.dot(a.astype(BF16), b.astype(BF16), preferred_element_type=F32)


def _dot_nt(a, b):
    return lax.dot_general(a.astype(BF16), b.astype(BF16), (((1,), (1,)), ((), ())),
                           preferred_element_type=F32)


def _dot_tn(a, b):
    return lax.dot_general(a.astype(BF16), b.astype(BF16), (((0,), (0,)), ((), ())),
                           preferred_element_type=F32)


def _rms(x, g):
    return x * lax.rsqrt(jnp.mean(x * x, axis=-1, keepdims=True) + NORM_EPS) * g


def _inproj_conv_kernel(x_ref, g_ref, w_ref, woc_ref, scw_ref, cfw_ref, cfp_ref, proj_ref, mixc_ref,
                        held, gb_buf, sc_buf, cf_buf, cf_sh, y_buf, *, tiles_per_seq):
    i = pl.program_id(0)
    tile = x_ref.shape[0]

    @pl.when(i == 0)
    def _():
        held[...] = jnp.zeros(held.shape, F32)
        sc_buf[0:SC_HALO, :] = jnp.zeros((SC_HALO, SC_WIDTH), F32)
        cf_buf[0:CF_HALO, :] = jnp.zeros((CF_HALO, CF_WIDTH), F32)

    col = lambda n: slice(n * SC_WIDTH, (n + 1) * SC_WIDTH)
    gb_buf[...] = held[:, col(0)]
    sc_buf[SC_HALO:, :] = held[:, col(1)] * held[:, col(2)]
    cf_buf[CF_HALO:, :] = held[:, col(3)] * jax.nn.sigmoid(held[:, col(4)])

    n_sh = cf_sh.shape[1]
    for res in range(1, SUBLANES):
        cf_sh[res - 1] = cf_buf[pl.ds(res, n_sh), :]

    h = _rms(x_ref[...], g_ref[0:1, :]).astype(BF16)

    def project(j):
        cols = slice(j * DOT_COLS, (j + 1) * DOT_COLS)
        out = jnp.dot(h, w_ref[:, cols], preferred_element_type=F32)
        if j < RWKV_COLS // DOT_COLS:
            proj_ref[:, cols] = out
        else:
            held[:, j * DOT_COLS - RWKV_COLS:(j + 1) * DOT_COLS - RWKV_COLS] = out

    def conv_rows(r0):
        acc = jnp.zeros((CONV_ROWS, SC_WIDTH), F32)
        for j in range(SC_KERNEL):
            acc = acc + scw_ref[j:j + 1, :] * sc_buf[pl.ds(r0 + SC_HALO - (SC_KERNEL - 1) + j, CONV_ROWS), :]
        y_buf[r0:r0 + CONV_ROWS, 0:SC_WIDTH] = (gb_buf[r0:r0 + CONV_ROWS, :] * acc).astype(y_buf.dtype)
        acc = jnp.zeros((CONV_ROWS, CF_WIDTH), F32) + cfp_ref[0:1, :]
        for j in range(CF_KERNEL):
            start = r0 + CF_HALO - (CF_KERNEL - 1) + j
            res = start % SUBLANES
            if res == 0:
                taps = cf_buf[pl.ds(start, CONV_ROWS), :]
            else:
                taps = cf_sh[res - 1, pl.ds(start - res, CONV_ROWS), :]
            acc = acc + cfw_ref[j:j + 1, :] * taps
        mu = jnp.mean(acc, axis=-1, keepdims=True)
        cen = acc - mu
        var = jnp.mean(cen * cen, axis=-1, keepdims=True)
        z = cen * lax.rsqrt(var + LN_EPS) * cfp_ref[1:2, :] + cfp_ref[2:3, :]
        y_buf[r0:r0 + CONV_ROWS, SC_WIDTH:] = (z * jax.nn.sigmoid(z)).astype(y_buf.dtype)

    n_conv = tile // CONV_ROWS
    done = 0
    for j in range(PROJ_WIDTH // DOT_COLS):
        project(j)
        upto = min(n_conv, done + CONV_BLOCKS_PER_DOT)
        for c in range(done, upto):
            conv_rows(c * CONV_ROWS)
            r1 = (c + 1) * CONV_ROWS
            if r1 % OUT_ROWS == 0:
                rows = slice(r1 - OUT_ROWS, r1)
                mixc_ref[rows, :] = jnp.dot(y_buf[rows, :], woc_ref[...], preferred_element_type=F32)
        done = upto
    assert done == n_conv

    keep = jnp.where(i % tiles_per_seq != 0, 1.0, 0.0).astype(F32)
    sc_buf[0:SC_HALO, :] = sc_buf[tile:tile + SC_HALO, :] * keep
    cf_buf[0:CF_HALO, :] = cf_buf[tile:tile + CF_HALO, :] * keep


def _layer_spec(a, layer, block=None, index=(0, 0), **kwargs):
    block = a.shape[1:] if block is None else block
    return pl.BlockSpec((None,) + tuple(block), lambda *_: (layer,) + tuple(index), **kwargs)


def _inproj_conv(x2, layer, gains, w, wo, sc_w, cf_w, cf_p, seq):
    rows = x2.shape[0]
    n_tiles = rows // ROW_TILE
    full = lambda a: _layer_spec(a, layer)
    cur = lambda i: (jnp.minimum(i, n_tiles - 1), 0)
    return pl.pallas_call(
        functools.partial(_inproj_conv_kernel, tiles_per_seq=seq // ROW_TILE),
        grid=(n_tiles + 1,),
        in_specs=[
            pl.BlockSpec((ROW_TILE, D_MODEL), cur),
            full(gains),
            _layer_spec(w, layer, pipeline_mode=pl.Buffered(1)),
            _layer_spec(wo, layer, (MIX_WIDTH - RWKV_WIDTH, D_MODEL), (1, 0)),
            full(sc_w), full(cf_w), full(cf_p),
        ],
        out_specs=[pl.BlockSpec((ROW_TILE, RWKV_COLS), cur),
                   pl.BlockSpec((ROW_TILE, D_MODEL), lambda i: (jnp.maximum(i - 1, 0), 0))],
        out_shape=[jax.ShapeDtypeStruct((rows, RWKV_COLS), F32),
                   jax.ShapeDtypeStruct((rows, D_MODEL), F32)],
        scratch_shapes=[pltpu.VMEM((ROW_TILE, CONV_COLS), F32),
                        pltpu.VMEM((ROW_TILE, SC_WIDTH), F32),
                        pltpu.VMEM((SC_HALO + ROW_TILE, SC_WIDTH), F32),
                        pltpu.VMEM((CF_HALO + ROW_TILE, CF_WIDTH), F32),
                        pltpu.VMEM((SUBLANES - 1, CF_HALO + ROW_TILE - SUBLANES, CF_WIDTH), F32),
                        pltpu.VMEM((ROW_TILE, SC_WIDTH + CF_WIDTH), BF16)],
        compiler_params=pltpu.CompilerParams(
            dimension_semantics=("arbitrary",), vmem_limit_bytes=VMEM_LIMIT),
        name="inproj_conv",
    )(x2, gains, w, wo, sc_w, cf_w, cf_p)


_MU_R, _MU_K, _MU_V, _W0, _A0, _K_K, _K_A, _R_K, _LNX_G, _LNX_B = range(10)
_PVEC_ROWS = 16


def _rwkv_kernel(r_ref, k_ref, v_ref, la_ref, lb_ref, pv_ref, lw_ref, o_ref):
    seq = r_ref.shape[0]
    slab = SLAB_CHUNKS * CHUNK
    n_pairs = RWKV_WIDTH // PAIR
    pv = pv_ref[...]
    prow = lambda i: pv[i:i + 1, :]
    w2 = lw_ref[0:DECAY_RANK, :]
    a2 = lw_ref[DECAY_RANK:DECAY_RANK + ICLR_RANK, :]
    g2 = lw_ref[DECAY_RANK + ICLR_RANK:, :]

    iota = lambda shape, dim: lax.broadcasted_iota(jnp.int32, shape, dim)
    srow, scol = iota((slab, slab), 0), iota((slab, slab), 1)
    tri_incl = ((srow // CHUNK == scol // CHUNK) & (srow >= scol)).astype(BF16)
    first_row = iota((slab, 1), 0) == 0
    head0 = iota((1, PAIR), 1) < HEAD_DIM
    head_c = (head0, jnp.logical_not(head0))
    row4 = iota((4 * CHUNK, PAIR), 0) % (2 * CHUNK)
    col4 = iota((4 * CHUNK, PAIR), 1) % CHUNK
    keep4 = jnp.where(row4 < CHUNK, row4, row4 - CHUNK + 1) > col4
    plane = iota((1, PAIR), 1) < CHUNK
    eye_p = iota((PAIR, PAIR), 0) == iota((PAIR, PAIR), 1)
    same_head = (iota((PAIR, PAIR), 0) // HEAD_DIM) == (iota((PAIR, PAIR), 1) // HEAD_DIM)
    zeros_b = jnp.zeros((CHUNK, PAIR), BF16)

    def head_sum(x):
        outs = []
        for p in range(x.shape[1] // PAIR):
            xp = x[:, p * PAIR:(p + 1) * PAIR]
            s0 = jnp.sum(jnp.where(head0, xp, 0.0), axis=-1, keepdims=True)
            s1 = jnp.sum(jnp.where(head0, 0.0, xp), axis=-1, keepdims=True)
            outs.append(jnp.where(head0, s0, s1))
        return outs[0] if len(outs) == 1 else jnp.concatenate(outs, axis=1)

    def slab_body(i, states):
        t0 = pl.multiple_of(i * slab, slab)
        rows = pl.ds(t0, slab)
        prev = pl.ds(jnp.maximum(t0 - 1, 0), 1)
        has_prev = jnp.where(i > 0, 1.0, 0.0).astype(F32)

        def with_shift(ref):
            cur = ref[rows, :]
            last = ref[prev, :] * has_prev
            return cur, jnp.where(first_row, last, pltpu.roll(cur, 1, 0))

        def mixed(ref, mu):
            cur, sh = with_shift(ref)
            return cur + (sh - cur) * mu

        r = mixed(r_ref, prow(_MU_R))
        k = mixed(k_ref, prow(_MU_K))
        v = mixed(v_ref, prow(_MU_V))
        _, lb_sh = with_shift(lb_ref)
        low = la_ref[rows, :] + lb_sh
        w_pre = prow(_W0) + _dot(jnp.tanh(low[:, :DECAY_RANK]), w2)
        log_decay = -DECAY_SCALE * jax.nn.sigmoid(w_pre)
        iclr = jax.nn.sigmoid(prow(_A0) + _dot(low[:, DECAY_RANK:DECAY_RANK + ICLR_RANK], a2))
        gate = _dot(jax.nn.sigmoid(low[:, DECAY_RANK + ICLR_RANK:]), g2)

        kk = k * prow(_K_K)
        kk = kk * lax.rsqrt(jnp.maximum(head_sum(kk * kk), 1e-24))
        k = k * (1.0 + (iclr - 1.0) * prow(_K_A))
        b_vec = kk * iclr

        ld_hi = log_decay.astype(BF16)
        ld_lo = (log_decay - ld_hi.astype(F32)).astype(BF16)
        cum = (jnp.dot(tri_incl, ld_hi, preferred_element_type=F32)
               + jnp.dot(tri_incl, ld_lo, preferred_element_type=F32))
        e_neg = jnp.exp(-cum)
        a_t = -kk * jnp.exp(cum - log_decay)
        r_t = r * jnp.exp(cum)
        b_t = b_vec * e_neg
        k_t = k * e_neg
        bonus = head_sum(r * k * prow(_R_K)) * v

        units = [(j, p) for j in range(SLAB_CHUNKS) for p in range(n_pairs)]
        heads = range(2)
        blk = lambda x, u: x[u[0] * CHUNK:(u[0] + 1) * CHUNK, u[1] * PAIR:(u[1] + 1) * PAIR]
        cum_end = {u: cum[(u[0] + 1) * CHUNK - 1:(u[0] + 1) * CHUNK, u[1] * PAIR:(u[1] + 1) * PAIR]
                   for u in units}
        at = {u: [jnp.where(head_c[h], blk(a_t, u), 0.0) for h in heads] for u in units}
        rtb = {u: blk(r_t, u).astype(BF16) for u in units}
        vb = {u: blk(v, u).astype(BF16) for u in units}
        rt = {u: [jnp.where(head_c[h], rtb[u], zeros_b) for h in heads] for u in units}
        vh = {u: [jnp.where(head_c[h], vb[u], zeros_b) for h in heads] for u in units}
        xab = {u: [at[u][h].astype(BF16) for h in heads] for u in units}
        big = {u: jnp.where(keep4,
                            _dot_nt(jnp.concatenate([xab[u][0], rt[u][0], xab[u][1], rt[u][1]], axis=0),
                                    jnp.concatenate([blk(b_t, u), blk(k_t, u)], axis=0)), 0.0)
               for u in units}
        bot = {u: [big[u][(2 * h + 1) * CHUNK:(2 * h + 2) * CHUNK] for h in heads] for u in units}
        px = {u: [big[u][2 * h * CHUNK:(2 * h + 1) * CHUNK] for h in heads] for u in units}
        xa = at
        pxb = {u: [px[u][h].astype(BF16) for h in heads] for u in units}
        for _ in range(6):
            res = {u: [_dot(pxb[u][h][:, :CHUNK], jnp.concatenate([pxb[u][h], xab[u][h]], axis=1))
                       for h in heads] for u in units}
            px = {u: [res[u][h][:, :PAIR] + jnp.where(plane, 0.0, px[u][h]) for h in heads] for u in units}
            xa = {u: [res[u][h][:, PAIR:] + xa[u][h] for h in heads] for u in units}
            pxb = {u: [px[u][h].astype(BF16) for h in heads] for u in units}
            xab = {u: [xa[u][h].astype(BF16) for h in heads] for u in units}
        u_v = {u: [_dot(pxb[u][h], jnp.concatenate([zeros_b, vh[u][h]], axis=0)) for h in heads]
               for u in units}
        ry = {u: [_dot(bot[u][h],
                       jnp.concatenate([jnp.concatenate([xab[u][h], u_v[u][h].astype(BF16)], axis=1),
                                        jnp.concatenate([zeros_b, vh[u][h]], axis=1)], axis=0))
                  for h in heads] for u in units}
        mn = {}
        for u in units:
            e_end = jnp.exp(cum_end[u] - blk(cum, u))
            bk_h = jnp.concatenate([blk(b_vec, u) * e_end, blk(k, u) * e_end], axis=0)
            apuv = jnp.concatenate(
                [jnp.concatenate([xa[u][0] + xa[u][1], u_v[u][0] + u_v[u][1]], axis=1).astype(BF16),
                 jnp.concatenate([zeros_b, vb[u]], axis=1)], axis=0)
            mn[u] = _dot_tn(bk_h, apuv)
        states = list(states)
        for u in units:
            j, p = u
            m = mn[u][:, :PAIR] + jnp.where(eye_p, jnp.exp(cum_end[u]), 0.0)
            r_p = blk(r_t, u) + ry[u][0][:, :PAIR] + ry[u][1][:, :PAIR]
            out = _dot(jnp.concatenate([m, r_p], axis=0), states[p])
            states[p] = jnp.where(same_head, out[:PAIR] + mn[u][:, PAIR:], 0.0)
            y = out[PAIR:] + ry[u][0][:, PAIR:] + ry[u][1][:, PAIR:]
            yc = y - head_sum(y) * (1.0 / HEAD_DIM)
            var = head_sum(yc * yc) * (1.0 / HEAD_DIM)
            lanes = slice(p * PAIR, (p + 1) * PAIR)
            y = yc * lax.rsqrt(var + GN_EPS) * prow(_LNX_G)[:, lanes] + prow(_LNX_B)[:, lanes]
            o_ref[pl.ds(t0 + j * CHUNK, CHUNK), lanes] = ((y + blk(bonus, u)) * blk(gate, u)).astype(o_ref.dtype)
        return tuple(states)

    zero_state = jnp.zeros((PAIR, PAIR), F32)
    lax.fori_loop(0, seq // slab, slab_body, (zero_state,) * n_pairs)


def _rwkv(proj3, layer, pvec, lowrank_w):
    bsz, seq, _ = proj3.shape
    col = lambda blk: pl.BlockSpec((None, seq, RWKV_WIDTH), lambda b, blk=blk: (b, 0, blk))
    low = lambda blk: pl.BlockSpec((None, seq, LOW_WIDTH), lambda b, blk=blk: (b, 0, blk))
    full = lambda a: _layer_spec(a, layer)
    return pl.pallas_call(
        _rwkv_kernel,
        grid=(bsz,),
        in_specs=[col(0), col(1), col(2),
                  low(3 * RWKV_WIDTH // LOW_WIDTH), low(3 * RWKV_WIDTH // LOW_WIDTH + 1),
                  full(pvec), full(lowrank_w)],
        out_specs=pl.BlockSpec((None, seq, RWKV_WIDTH), lambda b: (b, 0, 0)),
        out_shape=jax.ShapeDtypeStruct((bsz, seq, RWKV_WIDTH), BF16),
        compiler_params=pltpu.CompilerParams(
            dimension_semantics=("parallel",), vmem_limit_bytes=VMEM_LIMIT),
        name="rwkv",
    )(proj3, proj3, proj3, proj3, proj3, pvec, lowrank_w)


def _outffn_kernel(x_ref, yr_ref, mixc_ref, wo_ref, wg_ref, wu_ref, wd_ref, g_ref, o_ref):
    mix = jnp.dot(yr_ref[...], wo_ref[...], preferred_element_type=F32) + mixc_ref[...]
    x = x_ref[...] + _rms(mix, g_ref[1:2, :])
    h2 = _rms(x, g_ref[2:3, :]).astype(BF16)
    gate = jnp.dot(h2, wg_ref[...], preferred_element_type=F32)
    up = jnp.dot(h2, wu_ref[...], preferred_element_type=F32)
    act = (gate * jax.nn.sigmoid(gate) * up).astype(BF16)
    f = jnp.dot(act, wd_ref[...], preferred_element_type=F32)
    o_ref[...] = x + _rms(f, g_ref[3:4, :])


def _outffn(x2, y_rwkv, mix_conv, layer, wo, wg, wu, wd, gains):
    rows = x2.shape[0]
    tile = lambda a: pl.BlockSpec((ROW_TILE, a.shape[1]), lambda i: (i, 0))
    full = lambda a: _layer_spec(a, layer, pipeline_mode=pl.Buffered(1))
    return pl.pallas_call(
        _outffn_kernel,
        grid=(rows // ROW_TILE,),
        in_specs=[tile(x2), tile(y_rwkv), tile(mix_conv),
                  _layer_spec(wo, layer, (RWKV_WIDTH, D_MODEL), pipeline_mode=pl.Buffered(1)),
                  full(wg), full(wu), full(wd), full(gains)],
        out_specs=tile(x2),
        out_shape=jax.ShapeDtypeStruct(x2.shape, F32),
        compiler_params=pltpu.CompilerParams(
            dimension_semantics=("parallel",), vmem_limit_bytes=VMEM_LIMIT),
        name="outffn",
    )(x2, y_rwkv, mix_conv, wo, wg, wu, wd, gains)


def _fold_low_rank(mu, w):
    return (1.0 - mu)[..., None] * w, mu[..., None] * w


def kernel(x, w_in, mu_rkv, mu_wag, w0, w1, w2, a0, a1, a2, g1, g2, k_k, k_a, r_k, lnx_g, lnx_b,
           sc_conv_w, cf_conv_w, cf_conv_b, cf_ln_g, cf_ln_b, w_o, w_gate, w_up, w_down,
           pre_mix_g, post_mix_g, pre_ffn_g, post_ffn_g):
    bsz, seq, d = x.shape
    depth = w_in.shape[0]
    w1a, w1b = _fold_low_rank(mu_wag[:, 0], w1)
    a1a, a1b = _fold_low_rank(mu_wag[:, 1], a1)
    g1a, g1b = _fold_low_rank(mu_wag[:, 2], g1)
    w_cat = jnp.concatenate([w_in[:, :, :3 * RWKV_WIDTH], w1a, a1a, g1a, w1b, a1b, g1b,
                             w_in[:, :, 3 * RWKV_WIDTH:]], axis=2).astype(BF16)
    pvec = jnp.concatenate([
        mu_rkv.reshape(depth, 3, RWKV_WIDTH),
        jnp.stack([w0, a0, k_k, k_a, r_k.reshape(depth, RWKV_WIDTH), lnx_g, lnx_b], axis=1),
        jnp.zeros((depth, _PVEC_ROWS - 10, RWKV_WIDTH), F32)], axis=1)
    lowrank_w = jnp.concatenate([w2, a2, g2], axis=1).astype(BF16)
    cf_p = jnp.stack([cf_conv_b, cf_ln_g, cf_ln_b], axis=1)
    gains = jnp.stack([pre_mix_g, post_mix_g, pre_ffn_g, post_ffn_g], axis=1)
    wo, wg, wu, wd = (w.astype(BF16) for w in (w_o, w_gate, w_up, w_down))

    x2 = x.reshape(bsz * seq, d)
    for l in range(depth):
        proj, mix_conv = _inproj_conv(x2, l, gains, w_cat, wo, sc_conv_w, cf_conv_w, cf_p, seq)
        proj3 = proj.reshape(bsz, seq, RWKV_COLS)
        y_rwkv = _rwkv(proj3, l, pvec, lowrank_w).reshape(bsz * seq, RWKV_WIDTH)
        x2 = _outffn(x2, y_rwkv, mix_conv, l, wo, wg, wu, wd, gains)
    return x2.reshape(bsz, seq, d)
```

```python
import functools
import math

import jax
import jax.numpy as jnp
from jax import lax
from jax.experimental import pallas as pl
from jax.experimental.pallas import tpu as pltpu

D_MODEL = 1024
HEAD_DIM = 64
RWKV_WIDTH = 512
SC_WIDTH = 256
CF_WIDTH = 256
SC_KERNEL = 3
CF_KERNEL = 31
DECAY_RANK = 64
ICLR_RANK = 64
GATE_RANK = 128
LOW_WIDTH = DECAY_RANK + ICLR_RANK + GATE_RANK
IN_WIDTH = 3 * RWKV_WIDTH + 3 * SC_WIDTH + 2 * CF_WIDTH
PROJ_WIDTH = IN_WIDTH + 2 * LOW_WIDTH
NORM_EPS = 1e-6
LN_EPS = 1e-5
GN_EPS = 64e-5
MIX_WIDTH = RWKV_WIDTH + SC_WIDTH + CF_WIDTH
DECAY_SCALE = math.exp(-0.5)

CHUNK = 64
PAIR = 2 * HEAD_DIM
SLAB_CHUNKS = 2
ROW_TILE = 512
DOT_COLS = 256
CONV_COLS = 3 * SC_WIDTH + 2 * CF_WIDTH
RWKV_COLS = PROJ_WIDTH - CONV_COLS
CONV_ROWS = 32
CONV_BLOCKS_PER_DOT = 2
OUT_ROWS = 128
CF_HALO = 32
SC_HALO = 8
SUBLANES = 8
VMEM_LIMIT = 56 * 1024 * 1024

F32 = jnp.float32
BF16 = jnp.bfloat16


def _dot(a, b):
    return jnp.dot(a.astype(BF16), b.astype(BF16), preferred_element_type=F32)


def _dot_nt(a, b):
    return lax.dot_general(a.astype(BF16), b.astype(BF16), (((1,), (1,)), ((), ())),
                           preferred_element_type=F32)


def _dot_tn(a, b):
    return lax.dot_general(a.astype(BF16), b.astype(BF16), (((0,), (0,)), ((), ())),
                           preferred_element_type=F32)


def _rms(x, g):
    return x * lax.rsqrt(jnp.mean(x * x, axis=-1, keepdims=True) + NORM_EPS) * g


def _inproj_conv_kernel(x_ref, g_ref, w_ref, wl_ref, woc_ref, scw_ref, cfw_ref, cfp_ref, proj_ref, mixc_ref,
                        held, gb_buf, sc_buf, cf_buf, cf_sh, y_buf, *, tiles_per_seq):
    i = pl.program_id(0)
    tile = x_ref.shape[0]

    @pl.when(i == 0)
    def _():
        held[...] = jnp.zeros(held.shape, F32)
        sc_buf[0:SC_HALO, :] = jnp.zeros((SC_HALO, SC_WIDTH), F32)
        cf_buf[0:CF_HALO, :] = jnp.zeros((CF_HALO, CF_WIDTH), F32)

    col = lambda n: slice(n * SC_WIDTH, (n + 1) * SC_WIDTH)
    gb_buf[...] = held[:, col(0)]
    sc_buf[SC_HALO:, :] = held[:, col(1)] * held[:, col(2)]
    cf_buf[CF_HALO:, :] = held[:, col(3)] * jax.nn.sigmoid(held[:, col(4)])

    n_sh = cf_sh.shape[1]
    for res in range(1, SUBLANES):
        cf_sh[res - 1] = cf_buf[pl.ds(res, n_sh), :]

    h = _rms(x_ref[...], g_ref[0:1, :]).astype(BF16)

    def project(j):
        c0 = j * DOT_COLS
        cols = slice(c0, c0 + DOT_COLS)
        if c0 < 3 * RWKV_WIDTH:
            proj_ref[:, cols] = jnp.dot(h, w_ref[:, cols], preferred_element_type=F32)
        elif c0 < RWKV_COLS:
            proj_ref[:, cols] = jnp.dot(h, wl_ref[:, c0 - 3 * RWKV_WIDTH:c0 - 3 * RWKV_WIDTH + DOT_COLS],
                                        preferred_element_type=F32)
        else:
            src = c0 - 2 * LOW_WIDTH
            held[:, c0 - RWKV_COLS:c0 - RWKV_COLS + DOT_COLS] = jnp.dot(
                h, w_ref[:, src:src + DOT_COLS], preferred_element_type=F32)

    def conv_rows(r0):
        acc = jnp.zeros((CONV_ROWS, SC_WIDTH), F32)
        for j in range(SC_KERNEL):
            acc = acc + scw_ref[j:j + 1, :] * sc_buf[pl.ds(r0 + SC_HALO - (SC_KERNEL - 1) + j, CONV_ROWS), :]
        y_buf[r0:r0 + CONV_ROWS, 0:SC_WIDTH] = (gb_buf[r0:r0 + CONV_ROWS, :] * acc).astype(y_buf.dtype)
        acc = jnp.zeros((CONV_ROWS, CF_WIDTH), F32) + cfp_ref[0:1, :]
        for j in range(CF_KERNEL):
            start = r0 + CF_HALO - (CF_KERNEL - 1) + j
            res = start % SUBLANES
            if res == 0:
                taps = cf_buf[pl.ds(start, CONV_ROWS), :]
            else:
                taps = cf_sh[res - 1, pl.ds(start - res, CONV_ROWS), :]
            acc = acc + cfw_ref[j:j + 1, :] * taps
        mu = jnp.mean(acc, axis=-1, keepdims=True)
        cen = acc - mu
        var = jnp.mean(cen * cen, axis=-1, keepdims=True)
        z = cen * lax.rsqrt(var + LN_EPS) * cfp_ref[1:2, :] + cfp_ref[2:3, :]
        y_buf[r0:r0 + CONV_ROWS, SC_WIDTH:] = (z * jax.nn.sigmoid(z)).astype(y_buf.dtype)

    n_conv = tile // CONV_ROWS
    done = 0
    for j in range(PROJ_WIDTH // DOT_COLS):
        project(j)
        upto = min(n_conv, done + CONV_BLOCKS_PER_DOT)
        for c in range(done, upto):
            conv_rows(c * CONV_ROWS)
            r1 = (c + 1) * CONV_ROWS
            if r1 % OUT_ROWS == 0:
                rows = slice(r1 - OUT_ROWS, r1)
                mixc_ref[rows, :] = jnp.dot(y_buf[rows, :], woc_ref[...], preferred_element_type=F32)
        done = upto
    assert done == n_conv

    keep = jnp.where(i % tiles_per_seq != 0, 1.0, 0.0).astype(F32)
    sc_buf[0:SC_HALO, :] = sc_buf[tile:tile + SC_HALO, :] * keep
    cf_buf[0:CF_HALO, :] = cf_buf[tile:tile + CF_HALO, :] * keep


def _layer_spec(a, layer, block=None, index=(0, 0), **kwargs):
    block = a.shape[1:] if block is None else block
    return pl.BlockSpec((None,) + tuple(block), lambda *_: (layer,) + tuple(index), **kwargs)


def _inproj_conv(x2, layer, gains, w, w_low, wo, sc_w, cf_w, cf_p, seq):
    rows = x2.shape[0]
    n_tiles = rows // ROW_TILE
    full = lambda a: _layer_spec(a, layer)
    cur = lambda i: (jnp.minimum(i, n_tiles - 1), 0)
    return pl.pallas_call(
        functools.partial(_inproj_conv_kernel, tiles_per_seq=seq // ROW_TILE),
        grid=(n_tiles + 1,),
        in_specs=[
            pl.BlockSpec((ROW_TILE, D_MODEL), cur),
            full(gains),
            _layer_spec(w, layer, pipeline_mode=pl.Buffered(1)),
            _layer_spec(w_low, layer, pipeline_mode=pl.Buffered(1)),
            _layer_spec(wo, layer, (MIX_WIDTH - RWKV_WIDTH, D_MODEL), (1, 0)),
            full(sc_w), full(cf_w), full(cf_p),
        ],
        out_specs=[pl.BlockSpec((ROW_TILE, RWKV_COLS), cur),
                   pl.BlockSpec((ROW_TILE, D_MODEL), lambda i: (jnp.maximum(i - 1, 0), 0))],
        out_shape=[jax.ShapeDtypeStruct((rows, RWKV_COLS), F32),
                   jax.ShapeDtypeStruct((rows, D_MODEL), F32)],
        scratch_shapes=[pltpu.VMEM((ROW_TILE, CONV_COLS), F32),
                        pltpu.VMEM((ROW_TILE, SC_WIDTH), F32),
                        pltpu.VMEM((SC_HALO + ROW_TILE, SC_WIDTH), F32),
                        pltpu.VMEM((CF_HALO + ROW_TILE, CF_WIDTH), F32),
                        pltpu.VMEM((SUBLANES - 1, CF_HALO + ROW_TILE - SUBLANES, CF_WIDTH), F32),
                        pltpu.VMEM((ROW_TILE, SC_WIDTH + CF_WIDTH), BF16)],
        compiler_params=pltpu.CompilerParams(
            dimension_semantics=("arbitrary",), vmem_limit_bytes=VMEM_LIMIT),
        name="inproj_conv",
    )(x2, gains, w, w_low, wo, sc_w, cf_w, cf_p)


_MU_R, _MU_K, _MU_V, _W0, _A0, _K_K, _K_A, _R_K, _LNX_G, _LNX_B = range(10)
_PVEC_ROWS = 16


def _rwkv_kernel(r_ref, k_ref, v_ref, la_ref, lb_ref, pv_ref, lw_ref, o_ref, ld_s, cum_s, iclr_s, gate_s):
    seq = r_ref.shape[0]
    slab = SLAB_CHUNKS * CHUNK
    n_pairs = RWKV_WIDTH // PAIR
    n_slabs = seq // slab
    pv = pv_ref[...]
    prow = lambda i: pv[i:i + 1, :]
    w2 = lw_ref[0:DECAY_RANK, :]
    a2 = lw_ref[DECAY_RANK:DECAY_RANK + ICLR_RANK, :]
    g2 = lw_ref[DECAY_RANK + ICLR_RANK:, :]

    iota = lambda shape, dim: lax.broadcasted_iota(jnp.int32, shape, dim)
    srow, scol = iota((slab, slab), 0), iota((slab, slab), 1)
    tri_incl = ((srow // CHUNK == scol // CHUNK) & (srow >= scol)).astype(BF16)
    first_row = iota((slab, 1), 0) == 0
    head0 = iota((1, PAIR), 1) < HEAD_DIM
    head_c = (head0, jnp.logical_not(head0))
    row4 = iota((4 * CHUNK, PAIR), 0) % (2 * CHUNK)
    col4 = iota((4 * CHUNK, PAIR), 1) % CHUNK
    keep4 = jnp.where(row4 < CHUNK, row4, row4 - CHUNK + 1) > col4
    plane = iota((1, PAIR), 1) < CHUNK
    eye_p = iota((PAIR, PAIR), 0) == iota((PAIR, PAIR), 1)
    same_head = (iota((PAIR, PAIR), 0) // HEAD_DIM) == (iota((PAIR, PAIR), 1) // HEAD_DIM)
    zeros_b = jnp.zeros((CHUNK, PAIR), BF16)

    def head_sum(x):
        outs = []
        for p in range(x.shape[1] // PAIR):
            xp = x[:, p * PAIR:(p + 1) * PAIR]
            s0 = jnp.sum(jnp.where(head0, xp, 0.0), axis=-1, keepdims=True)
            s1 = jnp.sum(jnp.where(head0, 0.0, xp), axis=-1, keepdims=True)
            outs.append(jnp.where(head0, s0, s1))
        return outs[0] if len(outs) == 1 else jnp.concatenate(outs, axis=1)

    def with_shift(ref, i):
        t0 = pl.multiple_of(i * slab, slab)
        cur = ref[pl.ds(t0, slab), :]
        last = ref[pl.ds(jnp.maximum(t0 - 1, 0), 1), :] * jnp.where(i > 0, 1.0, 0.0).astype(F32)
        return cur, jnp.where(first_row, last, pltpu.roll(cur, 1, 0))

    def lowrank_stage(i):
        la, _ = with_shift(la_ref, i)
        _, lb_sh = with_shift(lb_ref, i)
        low = la + lb_sh
        w_pre = prow(_W0) + _dot(jnp.tanh(low[:, :DECAY_RANK]), w2)
        log_decay = -DECAY_SCALE * jax.nn.sigmoid(w_pre)
        ld_hi = log_decay.astype(BF16)
        ld_lo = (log_decay - ld_hi.astype(F32)).astype(BF16)
        ld_s[...] = log_decay
        cum_s[...] = (jnp.dot(tri_incl, ld_hi, preferred_element_type=F32)
                      + jnp.dot(tri_incl, ld_lo, preferred_element_type=F32))
        iclr_s[...] = jax.nn.sigmoid(prow(_A0) + _dot(low[:, DECAY_RANK:DECAY_RANK + ICLR_RANK], a2))
        gate_s[...] = _dot(jax.nn.sigmoid(low[:, DECAY_RANK + ICLR_RANK:]), g2)

    def slab_body(i, states):
        t0 = pl.multiple_of(i * slab, slab)

        def mixed(ref, mu):
            cur, sh = with_shift(ref, i)
            return cur + (sh - cur) * mu

        r = mixed(r_ref, prow(_MU_R))
        k = mixed(k_ref, prow(_MU_K))
        v = mixed(v_ref, prow(_MU_V))
        log_decay, cum, iclr = ld_s[...], cum_s[...], iclr_s[...]

        kk = k * prow(_K_K)
        kk = kk * lax.rsqrt(jnp.maximum(head_sum(kk * kk), 1e-24))
        k = k * (1.0 + (iclr - 1.0) * prow(_K_A))
        b_vec = kk * iclr

        e_neg = jnp.exp(-cum)
        a_t = -kk * jnp.exp(cum - log_decay)
        r_t = r * jnp.exp(cum)
        b_t = b_vec * e_neg
        k_t = k * e_neg
        bonus = head_sum(r * k * prow(_R_K)) * v

        units = [(j, p) for j in range(SLAB_CHUNKS) for p in range(n_pairs)]
        heads = range(2)
        blk = lambda x, u: x[u[0] * CHUNK:(u[0] + 1) * CHUNK, u[1] * PAIR:(u[1] + 1) * PAIR]
        cum_end = {u: cum[(u[0] + 1) * CHUNK - 1:(u[0] + 1) * CHUNK, u[1] * PAIR:(u[1] + 1) * PAIR]
                   for u in units}
        at = {u: [jnp.where(head_c[h], blk(a_t, u), 0.0) for h in heads] for u in units}
        rtb = {u: blk(r_t, u).astype(BF16) for u in units}
        vb = {u: blk(v, u).astype(BF16) for u in units}
        rt = {u: [jnp.where(head_c[h], rtb[u], zeros_b) for h in heads] for u in units}
        vh = {u: [jnp.where(head_c[h], vb[u], zeros_b) for h in heads] for u in units}
        xab = {u: [at[u][h].astype(BF16) for h in heads] for u in units}
        big = {u: jnp.where(keep4,
                            _dot_nt(jnp.concatenate([xab[u][0], rt[u][0], xab[u][1], rt[u][1]], axis=0),
                                    jnp.concatenate([blk(b_t, u), blk(k_t, u)], axis=0)), 0.0)
               for u in units}
        bot = {u: [big[u][(2 * h + 1) * CHUNK:(2 * h + 2) * CHUNK] for h in heads] for u in units}
        px = {u: [big[u][2 * h * CHUNK:(2 * h + 1) * CHUNK] for h in heads] for u in units}
        xa = at
        pxb = {u: [px[u][h].astype(BF16) for h in heads] for u in units}
        for _ in range(6):
            res = {u: [_dot(pxb[u][h][:, :CHUNK], jnp.concatenate([pxb[u][h], xab[u][h]], axis=1))
                       for h in heads] for u in units}
            px = {u: [res[u][h][:, :PAIR] + jnp.where(plane, 0.0, px[u][h]) for h in heads] for u in units}
            xa = {u: [res[u][h][:, PAIR:] + xa[u][h] for h in heads] for u in units}
            pxb = {u: [px[u][h].astype(BF16) for h in heads] for u in units}
            xab = {u: [xa[u][h].astype(BF16) for h in heads] for u in units}
        u_v = {u: [_dot(pxb[u][h], jnp.concatenate([zeros_b, vh[u][h]], axis=0)) for h in heads]
               for u in units}
        ry = {u: [_dot(bot[u][h],
                       jnp.concatenate([jnp.concatenate([xab[u][h], u_v[u][h].astype(BF16)], axis=1),
                                        jnp.concatenate([zeros_b, vh[u][h]], axis=1)], axis=0))
                  for h in heads] for u in units}
        mn = {}
        for u in units:
            e_end = jnp.exp(cum_end[u] - blk(cum, u))
            bk_h = jnp.concatenate([blk(b_vec, u) * e_end, blk(k, u) * e_end], axis=0)
            apuv = jnp.concatenate(
                [jnp.concatenate([xa[u][0] + xa[u][1], u_v[u][0] + u_v[u][1]], axis=1).astype(BF16),
                 jnp.concatenate([zeros_b, vb[u]], axis=1)], axis=0)
            mn[u] = _dot_tn(bk_h, apuv)
        states = list(states)
        gate = gate_s[...]
        for u in units:
            j, p = u
            m = mn[u][:, :PAIR] + jnp.where(eye_p, jnp.exp(cum_end[u]), 0.0)
            r_p = blk(r_t, u) + ry[u][0][:, :PAIR] + ry[u][1][:, :PAIR]
            out = _dot(jnp.concatenate([m, r_p], axis=0), states[p])
            states[p] = jnp.where(same_head, out[:PAIR] + mn[u][:, PAIR:], 0.0)
            y = out[PAIR:] + ry[u][0][:, PAIR:] + ry[u][1][:, PAIR:]
            yc = y - head_sum(y) * (1.0 / HEAD_DIM)
            var = head_sum(yc * yc) * (1.0 / HEAD_DIM)
            lanes = slice(p * PAIR, (p + 1) * PAIR)
            y = yc * lax.rsqrt(var + GN_EPS) * prow(_LNX_G)[:, lanes] + prow(_LNX_B)[:, lanes]
            o_ref[pl.ds(t0 + j * CHUNK, CHUNK), lanes] = ((y + blk(bonus, u)) * blk(gate, u)).astype(o_ref.dtype)
        lowrank_stage(jnp.minimum(i + 1, n_slabs - 1))
        return tuple(states)

    zero_state = jnp.zeros((PAIR, PAIR), F32)
    lowrank_stage(0)
    lax.fori_loop(0, n_slabs, slab_body, (zero_state,) * n_pairs)


def _rwkv(proj3, layer, pvec, lowrank_w):
    bsz, seq, _ = proj3.shape
    col = lambda blk: pl.BlockSpec((None, seq, RWKV_WIDTH), lambda b, blk=blk: (b, 0, blk))
    low = lambda blk: pl.BlockSpec((None, seq, LOW_WIDTH), lambda b, blk=blk: (b, 0, blk))
    full = lambda a: _layer_spec(a, layer)
    return pl.pallas_call(
        _rwkv_kernel,
        grid=(bsz,),
        in_specs=[col(0), col(1), col(2),
                  low(3 * RWKV_WIDTH // LOW_WIDTH), low(3 * RWKV_WIDTH // LOW_WIDTH + 1),
                  full(pvec), full(lowrank_w)],
        out_specs=pl.BlockSpec((None, seq, RWKV_WIDTH), lambda b: (b, 0, 0)),
        out_shape=jax.ShapeDtypeStruct((bsz, seq, RWKV_WIDTH), BF16),
        scratch_shapes=[pltpu.VMEM((SLAB_CHUNKS * CHUNK, RWKV_WIDTH), F32)] * 4,
        compiler_params=pltpu.CompilerParams(
            dimension_semantics=("parallel",), vmem_limit_bytes=VMEM_LIMIT),
        name="rwkv",
    )(proj3, proj3, proj3, proj3, proj3, pvec, lowrank_w)


def _outffn_kernel(x_ref, yr_ref, mixc_ref, wo_ref, wg_ref, wu_ref, wd_ref, g_ref, o_ref):
    mix = jnp.dot(yr_ref[...], wo_ref[...], preferred_element_type=F32) + mixc_ref[...]
    x = x_ref[...] + _rms(mix, g_ref[1:2, :])
    h2 = _rms(x, g_ref[2:3, :]).astype(BF16)
    gate = jnp.dot(h2, wg_ref[...], preferred_element_type=F32)
    up = jnp.dot(h2, wu_ref[...], preferred_element_type=F32)
    act = (gate * jax.nn.sigmoid(gate) * up).astype(BF16)
    f = jnp.dot(act, wd_ref[...], preferred_element_type=F32)
    o_ref[...] = x + _rms(f, g_ref[3:4, :])


def _outffn(x2, y_rwkv, mix_conv, layer, wo, wg, wu, wd, gains):
    rows = x2.shape[0]
    tile = lambda a: pl.BlockSpec((ROW_TILE, a.shape[1]), lambda i: (i, 0))
    full = lambda a: _layer_spec(a, layer, pipeline_mode=pl.Buffered(1))
    return pl.pallas_call(
        _outffn_kernel,
        grid=(rows // ROW_TILE,),
        in_specs=[tile(x2), tile(y_rwkv), tile(mix_conv),
                  _layer_spec(wo, layer, (RWKV_WIDTH, D_MODEL), pipeline_mode=pl.Buffered(1)),
                  full(wg), full(wu), full(wd), full(gains)],
        out_specs=tile(x2),
        out_shape=jax.ShapeDtypeStruct(x2.shape, F32),
        compiler_params=pltpu.CompilerParams(
            dimension_semantics=("parallel",), vmem_limit_bytes=VMEM_LIMIT),
        name="outffn",
    )(x2, y_rwkv, mix_conv, wo, wg, wu, wd, gains)


def _fold_low_rank(mu, w):
    return (1.0 - mu)[..., None] * w, mu[..., None] * w


def kernel(x, w_in, mu_rkv, mu_wag, w0, w1, w2, a0, a1, a2, g1, g2, k_k, k_a, r_k, lnx_g, lnx_b,
           sc_conv_w, cf_conv_w, cf_conv_b, cf_ln_g, cf_ln_b, w_o, w_gate, w_up, w_down,
           pre_mix_g, post_mix_g, pre_ffn_g, post_ffn_g):
    bsz, seq, d = x.shape
    depth = w_in.shape[0]
    w1a, w1b = _fold_low_rank(mu_wag[:, 0], w1)
    a1a, a1b = _fold_low_rank(mu_wag[:, 1], a1)
    g1a, g1b = _fold_low_rank(mu_wag[:, 2], g1)
    w_low = jnp.concatenate([w1a, a1a, g1a, w1b, a1b, g1b], axis=2).astype(BF16)
    w_inb = w_in.astype(BF16)
    pvec = jnp.concatenate([
        mu_rkv.reshape(depth, 3, RWKV_WIDTH),
        jnp.stack([w0, a0, k_k, k_a, r_k.reshape(depth, RWKV_WIDTH), lnx_g, lnx_b], axis=1),
        jnp.zeros((depth, _PVEC_ROWS - 10, RWKV_WIDTH), F32)], axis=1)
    lowrank_w = jnp.concatenate([w2, a2, g2], axis=1).astype(BF16)
    cf_p = jnp.stack([cf_conv_b, cf_ln_g, cf_ln_b], axis=1)
    gains = jnp.stack([pre_mix_g, post_mix_g, pre_ffn_g, post_ffn_g], axis=1)
    wo, wg, wu, wd = (w.astype(BF16) for w in (w_o, w_gate, w_up, w_down))

    x2 = x.reshape(bsz * seq, d)
    for l in range(depth):
        proj, mix_conv = _inproj_conv(x2, l, gains, w_inb, w_low, wo, sc_conv_w, cf_conv_w, cf_p, seq)
        proj3 = proj.reshape(bsz, seq, RWKV_COLS)
        y_rwkv = _rwkv(proj3, l, pvec, lowrank_w).reshape(bsz * seq, RWKV_WIDTH)
        x2 = _outffn(x2, y_rwkv, mix_conv, l, wo, wg, wu, wd, gains)
    return x2.reshape(bsz, seq, d)
```

```python
import functools
import math

import jax
import jax.numpy as jnp
from jax import lax
from jax.experimental import pallas as pl
from jax.experimental.pallas import tpu as pltpu

D_MODEL = 1024
HEAD_DIM = 64
RWKV_WIDTH = 512
SC_WIDTH = 256
CF_WIDTH = 256
SC_KERNEL = 3
CF_KERNEL = 31
DECAY_RANK = 64
ICLR_RANK = 64
GATE_RANK = 128
LOW_WIDTH = DECAY_RANK + ICLR_RANK + GATE_RANK
IN_WIDTH = 3 * RWKV_WIDTH + 3 * SC_WIDTH + 2 * CF_WIDTH
PROJ_WIDTH = IN_WIDTH + 2 * LOW_WIDTH
NORM_EPS = 1e-6
LN_EPS = 1e-5
GN_EPS = 64e-5
MIX_WIDTH = RWKV_WIDTH + SC_WIDTH + CF_WIDTH
DECAY_SCALE = math.exp(-0.5)

CHUNK = 64
PAIR = 2 * HEAD_DIM
SLAB_CHUNKS = 2
ROW_TILE = 512
DOT_COLS = 256
CONV_COLS = 3 * SC_WIDTH + 2 * CF_WIDTH
RWKV_COLS = PROJ_WIDTH - CONV_COLS
CONV_ROWS = 32
CONV_BLOCKS_PER_DOT = 2
OUT_ROWS = 128
CF_HALO = 32
SC_HALO = 8
SUBLANES = 8
BF16_ROWS = 16
VMEM_LIMIT = 56 * 1024 * 1024

F32 = jnp.float32
BF16 = jnp.bfloat16


def _dot(a, b):
    return jnp.dot(a.astype(BF16), b.astype(BF16), preferred_element_type=F32)


def _dot_nt(a, b):
    return lax.dot_general(a.astype(BF16), b.astype(BF16), (((1,), (1,)), ((), ())),
                           preferred_element_type=F32)


def _dot_tn(a, b):
    return lax.dot_general(a.astype(BF16), b.astype(BF16), (((0,), (0,)), ((), ())),
                           preferred_element_type=F32)


def _rms(x, g):
    return x * lax.rsqrt(jnp.mean(x * x, axis=-1, keepdims=True) + NORM_EPS) * g


def _inproj_conv_kernel(x_ref, g_ref, w_ref, wl_ref, woc_ref, scw_ref, cfw_ref, cfp_ref,
                        wg32_ref, wu32_ref, wd32_ref, proj_ref, mixc_ref, wg_ref, wu_ref, wd_ref,
                        held, gb_buf, sc_buf, cf_buf, cf_sh, y_buf, *, tiles_per_seq):
    i = pl.program_id(0)
    tile = x_ref.shape[0]
    wg_ref[...] = wg32_ref[...].astype(BF16)
    wu_ref[...] = wu32_ref[...].astype(BF16)
    wd_ref[...] = wd32_ref[...].astype(BF16)

    @pl.when(i == 0)
    def _():
        held[...] = jnp.zeros(held.shape, F32)
        sc_buf[0:SC_HALO, :] = jnp.zeros((SC_HALO, SC_WIDTH), F32)
        cf_buf[0:CF_HALO, :] = jnp.zeros((CF_HALO, CF_WIDTH), F32)

    col = lambda n: slice(n * SC_WIDTH, (n + 1) * SC_WIDTH)
    gb_buf[...] = held[:, col(0)]
    sc_buf[SC_HALO:, :] = held[:, col(1)] * held[:, col(2)]
    cf_buf[CF_HALO:, :] = held[:, col(3)] * jax.nn.sigmoid(held[:, col(4)])

    n_sh = cf_sh.shape[1]
    for res in range(1, SUBLANES):
        cf_sh[res - 1] = cf_buf[pl.ds(res, n_sh), :]

    h = _rms(x_ref[...], g_ref[0:1, :]).astype(BF16)

    def project(j):
        c0 = j * DOT_COLS
        cols = slice(c0, c0 + DOT_COLS)
        if c0 < 3 * RWKV_WIDTH:
            proj_ref[:, cols] = jnp.dot(h, w_ref[:, cols], preferred_element_type=F32)
        elif c0 < RWKV_COLS:
            proj_ref[:, cols] = jnp.dot(h, wl_ref[:, c0 - 3 * RWKV_WIDTH:c0 - 3 * RWKV_WIDTH + DOT_COLS],
                                        preferred_element_type=F32)
        else:
            src = c0 - 2 * LOW_WIDTH
            held[:, c0 - RWKV_COLS:c0 - RWKV_COLS + DOT_COLS] = jnp.dot(
                h, w_ref[:, src:src + DOT_COLS], preferred_element_type=F32)

    def conv_rows(r0):
        acc = jnp.zeros((CONV_ROWS, SC_WIDTH), F32)
        for j in range(SC_KERNEL):
            acc = acc + scw_ref[j:j + 1, :] * sc_buf[pl.ds(r0 + SC_HALO - (SC_KERNEL - 1) + j, CONV_ROWS), :]
        y_buf[r0:r0 + CONV_ROWS, 0:SC_WIDTH] = (gb_buf[r0:r0 + CONV_ROWS, :] * acc).astype(y_buf.dtype)
        acc = jnp.zeros((CONV_ROWS, CF_WIDTH), F32) + cfp_ref[0:1, :]
        for j in range(CF_KERNEL):
            start = r0 + CF_HALO - (CF_KERNEL - 1) + j
            res = start % SUBLANES
            if res == 0:
                taps = cf_buf[pl.ds(start, CONV_ROWS), :]
            else:
                taps = cf_sh[res - 1, pl.ds(start - res, CONV_ROWS), :]
            acc = acc + cfw_ref[j:j + 1, :] * taps
        mu = jnp.mean(acc, axis=-1, keepdims=True)
        cen = acc - mu
        var = jnp.mean(cen * cen, axis=-1, keepdims=True)
        z = cen * lax.rsqrt(var + LN_EPS) * cfp_ref[1:2, :] + cfp_ref[2:3, :]
        y_buf[r0:r0 + CONV_ROWS, SC_WIDTH:] = (z * jax.nn.sigmoid(z)).astype(y_buf.dtype)

    n_conv = tile // CONV_ROWS
    done = 0
    for j in range(PROJ_WIDTH // DOT_COLS):
        project(j)
        upto = min(n_conv, done + CONV_BLOCKS_PER_DOT)
        for c in range(done, upto):
            conv_rows(c * CONV_ROWS)
            r1 = (c + 1) * CONV_ROWS
            if r1 % OUT_ROWS == 0:
                rows = slice(r1 - OUT_ROWS, r1)
                mixc_ref[rows, :] = jnp.dot(y_buf[rows, :], woc_ref[...], preferred_element_type=F32)
        done = upto
    assert done == n_conv

    keep = jnp.where(i % tiles_per_seq != 0, 1.0, 0.0).astype(F32)
    sc_buf[0:SC_HALO, :] = sc_buf[tile:tile + SC_HALO, :] * keep
    cf_buf[0:CF_HALO, :] = cf_buf[tile:tile + CF_HALO, :] * keep


def _layer_spec(a, layer, block=None, index=(0, 0), **kwargs):
    block = a.shape[1:] if block is None else block
    return pl.BlockSpec((None,) + tuple(block), lambda *_: (layer,) + tuple(index), **kwargs)


def _inproj_conv(x2, layer, gains, w, w_low, wo, sc_w, cf_w, cf_p, w_gate, w_up, w_down, seq):
    rows = x2.shape[0]
    n_tiles = rows // ROW_TILE
    ffn = w_gate.shape[2]
    full = lambda a: _layer_spec(a, layer)
    cur = lambda i: (jnp.minimum(i, n_tiles - 1), 0)
    up_rows = D_MODEL // n_tiles
    assert up_rows * n_tiles == D_MODEL and up_rows % BF16_ROWS == 0 and ffn % DOT_COLS == 0
    down_blocks = ffn // DOT_COLS
    assert down_blocks <= n_tiles
    up_blk = lambda i: jnp.minimum(i, n_tiles - 1)
    down_blk = lambda i: jnp.minimum(i, down_blocks - 1)
    return pl.pallas_call(
        functools.partial(_inproj_conv_kernel, tiles_per_seq=seq // ROW_TILE),
        grid=(n_tiles + 1,),
        in_specs=[
            pl.BlockSpec((ROW_TILE, D_MODEL), cur),
            full(gains),
            pl.BlockSpec(w.shape, lambda i: (0, 0), pipeline_mode=pl.Buffered(1)),
            _layer_spec(w_low, layer, pipeline_mode=pl.Buffered(1)),
            _layer_spec(wo, layer, (MIX_WIDTH - RWKV_WIDTH, D_MODEL), (1, 0)),
            full(sc_w), full(cf_w), full(cf_p),
            pl.BlockSpec((None, up_rows, ffn), lambda i: (layer, up_blk(i), 0)),
            pl.BlockSpec((None, up_rows, ffn), lambda i: (layer, up_blk(i), 0)),
            pl.BlockSpec((None, DOT_COLS, D_MODEL), lambda i: (layer, down_blk(i), 0)),
        ],
        out_specs=[pl.BlockSpec((ROW_TILE, RWKV_COLS), cur),
                   pl.BlockSpec((ROW_TILE, D_MODEL), lambda i: (jnp.maximum(i - 1, 0), 0)),
                   pl.BlockSpec((up_rows, ffn), lambda i: (up_blk(i), 0)),
                   pl.BlockSpec((up_rows, ffn), lambda i: (up_blk(i), 0)),
                   pl.BlockSpec((DOT_COLS, D_MODEL), lambda i: (down_blk(i), 0))],
        out_shape=[jax.ShapeDtypeStruct((rows, RWKV_COLS), F32),
                   jax.ShapeDtypeStruct((rows, D_MODEL), F32),
                   jax.ShapeDtypeStruct((D_MODEL, ffn), BF16),
                   jax.ShapeDtypeStruct((D_MODEL, ffn), BF16),
                   jax.ShapeDtypeStruct((ffn, D_MODEL), BF16)],
        scratch_shapes=[pltpu.VMEM((ROW_TILE, CONV_COLS), F32),
                        pltpu.VMEM((ROW_TILE, SC_WIDTH), F32),
                        pltpu.VMEM((SC_HALO + ROW_TILE, SC_WIDTH), F32),
                        pltpu.VMEM((CF_HALO + ROW_TILE, CF_WIDTH), F32),
                        pltpu.VMEM((SUBLANES - 1, CF_HALO + ROW_TILE - SUBLANES, CF_WIDTH), F32),
                        pltpu.VMEM((ROW_TILE, SC_WIDTH + CF_WIDTH), BF16)],
        compiler_params=pltpu.CompilerParams(
            dimension_semantics=("arbitrary",), vmem_limit_bytes=VMEM_LIMIT),
        name="inproj_conv",
    )(x2, gains, w, w_low, wo, sc_w, cf_w, cf_p, w_gate, w_up, w_down)


_MU_R, _MU_K, _MU_V, _W0, _A0, _K_K, _K_A, _R_K, _LNX_G, _LNX_B = range(10)
_PVEC_ROWS = 16


def _rwkv_kernel(r_ref, k_ref, v_ref, la_ref, lb_ref, pv_ref, lw_ref, o_ref, ld_s, cum_s, iclr_s, gate_s):
    seq = r_ref.shape[0]
    slab = SLAB_CHUNKS * CHUNK
    n_pairs = RWKV_WIDTH // PAIR
    n_slabs = seq // slab
    pv = pv_ref[...]
    prow = lambda i: pv[i:i + 1, :]
    w2 = lw_ref[0:DECAY_RANK, :]
    a2 = lw_ref[DECAY_RANK:DECAY_RANK + ICLR_RANK, :]
    g2 = lw_ref[DECAY_RANK + ICLR_RANK:, :]

    iota = lambda shape, dim: lax.broadcasted_iota(jnp.int32, shape, dim)
    srow, scol = iota((slab, slab), 0), iota((slab, slab), 1)
    tri_incl = ((srow // CHUNK == scol // CHUNK) & (srow >= scol)).astype(BF16)
    first_row = iota((slab, 1), 0) == 0
    head0 = iota((1, PAIR), 1) < HEAD_DIM
    head_c = (head0, jnp.logical_not(head0))
    row4 = iota((4 * CHUNK, PAIR), 0) % (2 * CHUNK)
    col4 = iota((4 * CHUNK, PAIR), 1) % CHUNK
    keep4 = jnp.where(row4 < CHUNK, row4, row4 - CHUNK + 1) > col4
    plane = iota((1, PAIR), 1) < CHUNK
    eye_p = iota((PAIR, PAIR), 0) == iota((PAIR, PAIR), 1)
    same_head = (iota((PAIR, PAIR), 0) // HEAD_DIM) == (iota((PAIR, PAIR), 1) // HEAD_DIM)
    zeros_b = jnp.zeros((CHUNK, PAIR), BF16)

    def head_sum(x):
        outs = []
        for p in range(x.shape[1] // PAIR):
            xp = x[:, p * PAIR:(p + 1) * PAIR]
            s0 = jnp.sum(jnp.where(head0, xp, 0.0), axis=-1, keepdims=True)
            s1 = jnp.sum(jnp.where(head0, 0.0, xp), axis=-1, keepdims=True)
            outs.append(jnp.where(head0, s0, s1))
        return outs[0] if len(outs) == 1 else jnp.concatenate(outs, axis=1)

    def with_shift(ref, i):
        t0 = pl.multiple_of(i * slab, slab)
        cur = ref[pl.ds(t0, slab), :]
        last = ref[pl.ds(jnp.maximum(t0 - 1, 0), 1), :] * jnp.where(i > 0, 1.0, 0.0).astype(F32)
        return cur, jnp.where(first_row, last, pltpu.roll(cur, 1, 0))

    def lowrank_stage(i):
        la, _ = with_shift(la_ref, i)
        _, lb_sh = with_shift(lb_ref, i)
        low = la + lb_sh
        w_pre = prow(_W0) + _dot(jnp.tanh(low[:, :DECAY_RANK]), w2)
        log_decay = -DECAY_SCALE * jax.nn.sigmoid(w_pre)
        ld_hi = log_decay.astype(BF16)
        ld_lo = (log_decay - ld_hi.astype(F32)).astype(BF16)
        ld_s[...] = log_decay
        cum_s[...] = (jnp.dot(tri_incl, ld_hi, preferred_element_type=F32)
                      + jnp.dot(tri_incl, ld_lo, preferred_element_type=F32))
        iclr_s[...] = jax.nn.sigmoid(prow(_A0) + _dot(low[:, DECAY_RANK:DECAY_RANK + ICLR_RANK], a2))
        gate_s[...] = _dot(jax.nn.sigmoid(low[:, DECAY_RANK + ICLR_RANK:]), g2)

    def slab_body(i, states):
        t0 = pl.multiple_of(i * slab, slab)

        def mixed(ref, mu):
            cur, sh = with_shift(ref, i)
            return cur + (sh - cur) * mu

        r = mixed(r_ref, prow(_MU_R))
        k = mixed(k_ref, prow(_MU_K))
        v = mixed(v_ref, prow(_MU_V))
        log_decay, cum, iclr = ld_s[...], cum_s[...], iclr_s[...]

        kk = k * prow(_K_K)
        kk = kk * lax.rsqrt(jnp.maximum(head_sum(kk * kk), 1e-24))
        k = k * (1.0 + (iclr - 1.0) * prow(_K_A))
        b_vec = kk * iclr

        e_neg = jnp.exp(-cum)
        a_t = -kk * jnp.exp(cum - log_decay)
        r_t = r * jnp.exp(cum)
        b_t = b_vec * e_neg
        k_t = k * e_neg
        bonus = head_sum(r * k * prow(_R_K)) * v

        units = [(j, p) for j in range(SLAB_CHUNKS) for p in range(n_pairs)]
        heads = range(2)
        blk = lambda x, u: x[u[0] * CHUNK:(u[0] + 1) * CHUNK, u[1] * PAIR:(u[1] + 1) * PAIR]
        cum_end = {u: cum[(u[0] + 1) * CHUNK - 1:(u[0] + 1) * CHUNK, u[1] * PAIR:(u[1] + 1) * PAIR]
                   for u in units}
        at = {u: [jnp.where(head_c[h], blk(a_t, u), 0.0) for h in heads] for u in units}
        rtb = {u: blk(r_t, u).astype(BF16) for u in units}
        vb = {u: blk(v, u).astype(BF16) for u in units}
        rt = {u: [jnp.where(head_c[h], rtb[u], zeros_b) for h in heads] for u in units}
        vh = {u: [jnp.where(head_c[h], vb[u], zeros_b) for h in heads] for u in units}
        xab = {u: [at[u][h].astype(BF16) for h in heads] for u in units}
        big = {u: jnp.where(keep4,
                            _dot_nt(jnp.concatenate([xab[u][0], rt[u][0], xab[u][1], rt[u][1]], axis=0),
                                    jnp.concatenate([blk(b_t, u), blk(k_t, u)], axis=0)), 0.0)
               for u in units}
        bot = {u: [big[u][(2 * h + 1) * CHUNK:(2 * h + 2) * CHUNK] for h in heads] for u in units}
        px = {u: [big[u][2 * h * CHUNK:(2 * h + 1) * CHUNK] for h in heads] for u in units}
        xa = at
        pxb = {u: [px[u][h].astype(BF16) for h in heads] for u in units}
        for _ in range(6):
            res = {u: [_dot(pxb[u][h][:, :CHUNK], jnp.concatenate([pxb[u][h], xab[u][h]], axis=1))
                       for h in heads] for u in units}
            px = {u: [res[u][h][:, :PAIR] + jnp.where(plane, 0.0, px[u][h]) for h in heads] for u in units}
            xa = {u: [res[u][h][:, PAIR:] + xa[u][h] for h in heads] for u in units}
            pxb = {u: [px[u][h].astype(BF16) for h in heads] for u in units}
            xab = {u: [xa[u][h].astype(BF16) for h in heads] for u in units}
        u_v = {u: [_dot(pxb[u][h], jnp.concatenate([zeros_b, vh[u][h]], axis=0)) for h in heads]
               for u in units}
        ry = {u: [_dot(bot[u][h],
                       jnp.concatenate([jnp.concatenate([xab[u][h], u_v[u][h].astype(BF16)], axis=1),
                                        jnp.concatenate([zeros_b, vh[u][h]], axis=1)], axis=0))
                  for h in heads] for u in units}
        mn = {}
        for u in units:
            e_end = jnp.exp(cum_end[u] - blk(cum, u))
            bk_h = jnp.concatenate([blk(b_vec, u) * e_end, blk(k, u) * e_end], axis=0)
            apuv = jnp.concatenate(
                [jnp.concatenate([xa[u][0] + xa[u][1], u_v[u][0] + u_v[u][1]], axis=1).astype(BF16),
                 jnp.concatenate([zeros_b, vb[u]], axis=1)], axis=0)
            mn[u] = _dot_tn(bk_h, apuv)
        states = list(states)
        gate = gate_s[...]
        for u in units:
            j, p = u
            m = mn[u][:, :PAIR] + jnp.where(eye_p, jnp.exp(cum_end[u]), 0.0)
            r_p = blk(r_t, u) + ry[u][0][:, :PAIR] + ry[u][1][:, :PAIR]
            out = _dot(jnp.concatenate([m, r_p], axis=0), states[p])
            states[p] = jnp.where(same_head, out[:PAIR] + mn[u][:, PAIR:], 0.0)
            y = out[PAIR:] + ry[u][0][:, PAIR:] + ry[u][1][:, PAIR:]
            yc = y - head_sum(y) * (1.0 / HEAD_DIM)
            var = head_sum(yc * yc) * (1.0 / HEAD_DIM)
            lanes = slice(p * PAIR, (p + 1) * PAIR)
            y = yc * lax.rsqrt(var + GN_EPS) * prow(_LNX_G)[:, lanes] + prow(_LNX_B)[:, lanes]
            o_ref[pl.ds(t0 + j * CHUNK, CHUNK), lanes] = ((y + blk(bonus, u)) * blk(gate, u)).astype(o_ref.dtype)
        lowrank_stage(jnp.minimum(i + 1, n_slabs - 1))
        return tuple(states)

    zero_state = jnp.zeros((PAIR, PAIR), F32)
    lowrank_stage(0)
    lax.fori_loop(0, n_slabs, slab_body, (zero_state,) * n_pairs)


def _rwkv(proj3, layer, pvec, lowrank_w):
    bsz, seq, _ = proj3.shape
    col = lambda blk: pl.BlockSpec((None, seq, RWKV_WIDTH), lambda b, blk=blk: (b, 0, blk))
    low = lambda blk: pl.BlockSpec((None, seq, LOW_WIDTH), lambda b, blk=blk: (b, 0, blk))
    full = lambda a: _layer_spec(a, layer)
    return pl.pallas_call(
        _rwkv_kernel,
        grid=(bsz,),
        in_specs=[col(0), col(1), col(2),
                  low(3 * RWKV_WIDTH // LOW_WIDTH), low(3 * RWKV_WIDTH // LOW_WIDTH + 1),
                  full(pvec), full(lowrank_w)],
        out_specs=pl.BlockSpec((None, seq, RWKV_WIDTH), lambda b: (b, 0, 0)),
        out_shape=jax.ShapeDtypeStruct((bsz, seq, RWKV_WIDTH), BF16),
        scratch_shapes=[pltpu.VMEM((SLAB_CHUNKS * CHUNK, RWKV_WIDTH), F32)] * 4,
        compiler_params=pltpu.CompilerParams(
            dimension_semantics=("parallel",), vmem_limit_bytes=VMEM_LIMIT),
        name="rwkv",
    )(proj3, proj3, proj3, proj3, proj3, pvec, lowrank_w)


def _outffn_kernel(x_ref, yr_ref, mixc_ref, wo_ref, wg_ref, wu_ref, wd_ref, g_ref, wn32_ref, o_ref, wn_ref):
    wn_ref[...] = wn32_ref[...].astype(BF16)
    half = x_ref.shape[0] // 2
    halves = [slice(0, half), slice(half, 2 * half)]
    mix = [jnp.dot(yr_ref[s, :], wo_ref[...], preferred_element_type=F32) + mixc_ref[s, :] for s in halves]
    x = [x_ref[s, :] + _rms(m, g_ref[1:2, :]) for s, m in zip(halves, mix)]
    h2 = [_rms(xs, g_ref[2:3, :]).astype(BF16) for xs in x]
    act = []
    for h in h2:
        gate = jnp.dot(h, wg_ref[...], preferred_element_type=F32)
        up = jnp.dot(h, wu_ref[...], preferred_element_type=F32)
        act.append((gate * jax.nn.sigmoid(gate) * up).astype(BF16))
    f = [jnp.dot(a, wd_ref[...], preferred_element_type=F32) for a in act]
    for s, xs, fs in zip(halves, x, f):
        o_ref[s, :] = xs + _rms(fs, g_ref[3:4, :])


def _outffn(x2, y_rwkv, mix_conv, layer, wo, wg, wu, wd, gains, w_in, next_layer):
    rows = x2.shape[0]
    n_tiles = rows // ROW_TILE
    in_rows = D_MODEL // n_tiles
    assert in_rows * n_tiles == D_MODEL and in_rows % BF16_ROWS == 0
    tile = lambda a: pl.BlockSpec((ROW_TILE, a.shape[1]), lambda i: (i, 0))
    whole = lambda a: pl.BlockSpec(a.shape, lambda i: (0, 0), pipeline_mode=pl.Buffered(1))
    return pl.pallas_call(
        _outffn_kernel,
        grid=(n_tiles,),
        in_specs=[tile(x2), tile(y_rwkv), tile(mix_conv),
                  _layer_spec(wo, layer, (RWKV_WIDTH, D_MODEL), pipeline_mode=pl.Buffered(1)),
                  whole(wg), whole(wu), whole(wd),
                  _layer_spec(gains, layer, pipeline_mode=pl.Buffered(1)),
                  pl.BlockSpec((None, in_rows, w_in.shape[2]), lambda i: (next_layer, i, 0))],
        out_specs=[tile(x2), pl.BlockSpec((in_rows, w_in.shape[2]), lambda i: (i, 0))],
        out_shape=[jax.ShapeDtypeStruct(x2.shape, F32),
                   jax.ShapeDtypeStruct(w_in.shape[1:], BF16)],
        compiler_params=pltpu.CompilerParams(
            dimension_semantics=("parallel",), vmem_limit_bytes=VMEM_LIMIT),
        name="outffn",
    )(x2, y_rwkv, mix_conv, wo, wg, wu, wd, gains, w_in)


def _fold_low_rank(mu, w):
    return (1.0 - mu)[..., None] * w, mu[..., None] * w


def kernel(x, w_in, mu_rkv, mu_wag, w0, w1, w2, a0, a1, a2, g1, g2, k_k, k_a, r_k, lnx_g, lnx_b,
           sc_conv_w, cf_conv_w, cf_conv_b, cf_ln_g, cf_ln_b, w_o, w_gate, w_up, w_down,
           pre_mix_g, post_mix_g, pre_ffn_g, post_ffn_g):
    bsz, seq, d = x.shape
    depth = w_in.shape[0]
    w1a, w1b = _fold_low_rank(mu_wag[:, 0], w1)
    a1a, a1b = _fold_low_rank(mu_wag[:, 1], a1)
    g1a, g1b = _fold_low_rank(mu_wag[:, 2], g1)
    w_low = jnp.concatenate([w1a, a1a, g1a, w1b, a1b, g1b], axis=2).astype(BF16)
    w_inb = w_in[0].astype(BF16)
    pvec = jnp.concatenate([
        mu_rkv.reshape(depth, 3, RWKV_WIDTH),
        jnp.stack([w0, a0, k_k, k_a, r_k.reshape(depth, RWKV_WIDTH), lnx_g, lnx_b], axis=1),
        jnp.zeros((depth, _PVEC_ROWS - 10, RWKV_WIDTH), F32)], axis=1)
    lowrank_w = jnp.concatenate([w2, a2, g2], axis=1).astype(BF16)
    cf_p = jnp.stack([cf_conv_b, cf_ln_g, cf_ln_b], axis=1)
    gains = jnp.stack([pre_mix_g, post_mix_g, pre_ffn_g, post_ffn_g], axis=1)
    wo = w_o.astype(BF16)

    x2 = x.reshape(bsz * seq, d)
    for l in range(depth):
        proj, mix_conv, wg, wu, wd = _inproj_conv(x2, l, gains, w_inb, w_low, wo, sc_conv_w, cf_conv_w, cf_p,
                                                  w_gate, w_up, w_down, seq)
        proj3 = proj.reshape(bsz, seq, RWKV_COLS)
        y_rwkv = _rwkv(proj3, l, pvec, lowrank_w).reshape(bsz * seq, RWKV_WIDTH)
        x2, w_inb = _outffn(x2, y_rwkv, mix_conv, l, wo, wg, wu, wd, gains, w_in, min(l + 1, depth - 1))
    return x2.reshape(bsz, seq, d)
```

```python
import functools
import math

import jax
import jax.numpy as jnp
from jax import lax
from jax.experimental import pallas as pl
from jax.experimental.pallas import tpu as pltpu

D_MODEL = 1024
HEAD_DIM = 64
RWKV_WIDTH = 512
SC_WIDTH = 256
CF_WIDTH = 256
SC_KERNEL = 3
CF_KERNEL = 31
DECAY_RANK = 64
ICLR_RANK = 64
GATE_RANK = 128
LOW_WIDTH = DECAY_RANK + ICLR_RANK + GATE_RANK
IN_WIDTH = 3 * RWKV_WIDTH + 3 * SC_WIDTH + 2 * CF_WIDTH
PROJ_WIDTH = IN_WIDTH + 2 * LOW_WIDTH
NORM_EPS = 1e-6
LN_EPS = 1e-5
GN_EPS = 64e-5
MIX_WIDTH = RWKV_WIDTH + SC_WIDTH + CF_WIDTH
DECAY_SCALE = math.exp(-0.5)

CHUNK = 64
PAIR = 2 * HEAD_DIM
SLAB_CHUNKS = 2
ROW_TILE = 512
DOT_COLS = 256
CONV_COLS = 3 * SC_WIDTH + 2 * CF_WIDTH
RWKV_COLS = PROJ_WIDTH - CONV_COLS
CONV_ROWS = 64
CONV_BLOCKS_PER_DOT = 1
CONV_FIRST_DOT = 3
OUT_ROWS = 128
FFN_PARTS = 2
CF_HALO = 32
SC_HALO = 8
SUBLANES = 8
BF16_ROWS = 16
VMEM_LIMIT = 56 * 1024 * 1024

F32 = jnp.float32
BF16 = jnp.bfloat16


def _dot(a, b):
    return jnp.dot(a.astype(BF16), b.astype(BF16), preferred_element_type=F32)


def _dot_nt(a, b):
    return lax.dot_general(a.astype(BF16), b.astype(BF16), (((1,), (1,)), ((), ())),
                           preferred_element_type=F32)


def _dot_tn(a, b):
    return lax.dot_general(a.astype(BF16), b.astype(BF16), (((0,), (0,)), ((), ())),
                           preferred_element_type=F32)


def _rms(x, g):
    return x * lax.rsqrt(jnp.mean(x * x, axis=-1, keepdims=True) + NORM_EPS) * g


def _inproj_conv_kernel(x_ref, g_ref, w_ref, wl_ref, woc_ref, scw_ref, cfw_ref, cfp_ref,
                        wg32_ref, wu32_ref, wd32_ref, proj_ref, mixc_ref, wg_ref, wu_ref, wd_ref,
                        held, gb_buf, sc_buf, cf_buf, cf_sh, y_buf, *, tiles_per_seq):
    i = pl.program_id(0)
    tile = x_ref.shape[0]
    wg_ref[...] = wg32_ref[...].astype(BF16)
    wu_ref[...] = wu32_ref[...].astype(BF16)
    wd_ref[...] = wd32_ref[...].astype(BF16)

    @pl.when(i == 0)
    def _():
        held[...] = jnp.zeros(held.shape, F32)
        sc_buf[0:SC_HALO, :] = jnp.zeros((SC_HALO, SC_WIDTH), F32)
        cf_buf[0:CF_HALO, :] = jnp.zeros((CF_HALO, CF_WIDTH), F32)

    col = lambda n: slice(n * SC_WIDTH, (n + 1) * SC_WIDTH)
    gb_buf[...] = held[:, col(0)]
    sc_buf[SC_HALO:, :] = held[:, col(1)] * held[:, col(2)]
    cf_buf[CF_HALO:, :] = held[:, col(3)] * jax.nn.sigmoid(held[:, col(4)])

    n_sh = cf_sh.shape[1]
    for res in range(1, SUBLANES):
        cf_sh[res - 1] = cf_buf[pl.ds(res, n_sh), :]

    h = _rms(x_ref[...], g_ref[0:1, :]).astype(BF16)

    def project(j):
        c0 = j * DOT_COLS
        cols = slice(c0, c0 + DOT_COLS)
        if c0 < 3 * RWKV_WIDTH:
            proj_ref[:, cols] = jnp.dot(h, w_ref[:, cols], preferred_element_type=F32)
        elif c0 < RWKV_COLS:
            proj_ref[:, cols] = jnp.dot(h, wl_ref[:, c0 - 3 * RWKV_WIDTH:c0 - 3 * RWKV_WIDTH + DOT_COLS],
                                        preferred_element_type=F32)
        else:
            src = c0 - 2 * LOW_WIDTH
            held[:, c0 - RWKV_COLS:c0 - RWKV_COLS + DOT_COLS] = jnp.dot(
                h, w_ref[:, src:src + DOT_COLS], preferred_element_type=F32)

    def conv_rows(r0):
        acc = jnp.zeros((CONV_ROWS, SC_WIDTH), F32)
        for j in range(SC_KERNEL):
            acc = acc + scw_ref[j:j + 1, :] * sc_buf[pl.ds(r0 + SC_HALO - (SC_KERNEL - 1) + j, CONV_ROWS), :]
        y_buf[r0:r0 + CONV_ROWS, 0:SC_WIDTH] = (gb_buf[r0:r0 + CONV_ROWS, :] * acc).astype(y_buf.dtype)
        acc = jnp.zeros((CONV_ROWS, CF_WIDTH), F32) + cfp_ref[0:1, :]
        for j in range(CF_KERNEL):
            start = r0 + CF_HALO - (CF_KERNEL - 1) + j
            res = start % SUBLANES
            if res == 0:
                taps = cf_buf[pl.ds(start, CONV_ROWS), :]
            else:
                taps = cf_sh[res - 1, pl.ds(start - res, CONV_ROWS), :]
            acc = acc + cfw_ref[j:j + 1, :] * taps
        mu = jnp.mean(acc, axis=-1, keepdims=True)
        cen = acc - mu
        var = jnp.mean(cen * cen, axis=-1, keepdims=True)
        z = cen * lax.rsqrt(var + LN_EPS) * cfp_ref[1:2, :] + cfp_ref[2:3, :]
        y_buf[r0:r0 + CONV_ROWS, SC_WIDTH:] = (z * jax.nn.sigmoid(z)).astype(y_buf.dtype)

    n_conv = tile // CONV_ROWS
    done = 0
    for j in range(PROJ_WIDTH // DOT_COLS):
        project(j)
        upto = min(n_conv, done + CONV_BLOCKS_PER_DOT) if j >= CONV_FIRST_DOT else done
        for c in range(done, upto):
            conv_rows(c * CONV_ROWS)
            r1 = (c + 1) * CONV_ROWS
            if r1 % OUT_ROWS == 0:
                rows = slice(r1 - OUT_ROWS, r1)
                mixc_ref[rows, :] = jnp.dot(y_buf[rows, :], woc_ref[...], preferred_element_type=F32)
        done = upto
    assert done == n_conv

    keep = jnp.where(i % tiles_per_seq != 0, 1.0, 0.0).astype(F32)
    sc_buf[0:SC_HALO, :] = sc_buf[tile:tile + SC_HALO, :] * keep
    cf_buf[0:CF_HALO, :] = cf_buf[tile:tile + CF_HALO, :] * keep


def _layer_spec(a, layer, block=None, index=(0, 0), **kwargs):
    block = a.shape[1:] if block is None else block
    return pl.BlockSpec((None,) + tuple(block), lambda *_: (layer,) + tuple(index), **kwargs)


def _inproj_conv(x2, layer, gains, w, w_low, wo, sc_w, cf_w, cf_p, w_gate, w_up, w_down, seq):
    rows = x2.shape[0]
    n_tiles = rows // ROW_TILE
    ffn = w_gate.shape[2]
    full = lambda a: _layer_spec(a, layer)
    cur = lambda i: (jnp.minimum(i, n_tiles - 1), 0)
    up_rows = D_MODEL // n_tiles
    assert up_rows * n_tiles == D_MODEL and up_rows % BF16_ROWS == 0 and ffn % DOT_COLS == 0
    down_blocks = ffn // DOT_COLS
    assert down_blocks <= n_tiles
    up_blk = lambda i: jnp.minimum(i, n_tiles - 1)
    down_blk = lambda i: jnp.minimum(i, down_blocks - 1)
    return pl.pallas_call(
        functools.partial(_inproj_conv_kernel, tiles_per_seq=seq // ROW_TILE),
        grid=(n_tiles + 1,),
        in_specs=[
            pl.BlockSpec((ROW_TILE, D_MODEL), cur),
            full(gains),
            pl.BlockSpec(w.shape, lambda i: (0, 0), pipeline_mode=pl.Buffered(1)),
            _layer_spec(w_low, layer, pipeline_mode=pl.Buffered(1)),
            _layer_spec(wo, layer, (MIX_WIDTH - RWKV_WIDTH, D_MODEL), (1, 0)),
            full(sc_w), full(cf_w), full(cf_p),
            pl.BlockSpec((None, up_rows, ffn), lambda i: (layer, up_blk(i), 0)),
            pl.BlockSpec((None, up_rows, ffn), lambda i: (layer, up_blk(i), 0)),
            pl.BlockSpec((None, DOT_COLS, D_MODEL), lambda i: (layer, down_blk(i), 0)),
        ],
        out_specs=[pl.BlockSpec((ROW_TILE, RWKV_COLS), cur),
                   pl.BlockSpec((ROW_TILE, D_MODEL), lambda i: (jnp.maximum(i - 1, 0), 0)),
                   pl.BlockSpec((up_rows, ffn), lambda i: (up_blk(i), 0)),
                   pl.BlockSpec((up_rows, ffn), lambda i: (up_blk(i), 0)),
                   pl.BlockSpec((DOT_COLS, D_MODEL), lambda i: (down_blk(i), 0))],
        out_shape=[jax.ShapeDtypeStruct((rows, RWKV_COLS), F32),
                   jax.ShapeDtypeStruct((rows, D_MODEL), F32),
                   jax.ShapeDtypeStruct((D_MODEL, ffn), BF16),
                   jax.ShapeDtypeStruct((D_MODEL, ffn), BF16),
                   jax.ShapeDtypeStruct((ffn, D_MODEL), BF16)],
        scratch_shapes=[pltpu.VMEM((ROW_TILE, CONV_COLS), F32),
                        pltpu.VMEM((ROW_TILE, SC_WIDTH), F32),
                        pltpu.VMEM((SC_HALO + ROW_TILE, SC_WIDTH), F32),
                        pltpu.VMEM((CF_HALO + ROW_TILE, CF_WIDTH), F32),
                        pltpu.VMEM((SUBLANES - 1, CF_HALO + ROW_TILE - SUBLANES, CF_WIDTH), F32),
                        pltpu.VMEM((ROW_TILE, SC_WIDTH + CF_WIDTH), BF16)],
        compiler_params=pltpu.CompilerParams(
            dimension_semantics=("arbitrary",), vmem_limit_bytes=VMEM_LIMIT),
        name="inproj_conv",
    )(x2, gains, w, w_low, wo, sc_w, cf_w, cf_p, w_gate, w_up, w_down)


_MU_R, _MU_K, _MU_V, _W0, _A0, _K_K, _K_A, _R_K, _LNX_G, _LNX_B = range(10)
_PVEC_ROWS = 16


def _rwkv_kernel(r_ref, k_ref, v_ref, la_ref, lb_ref, pv_ref, lw_ref, o_ref, ld_s, cum_s, iclr_s, gate_s):
    seq = r_ref.shape[0]
    slab = SLAB_CHUNKS * CHUNK
    n_pairs = RWKV_WIDTH // PAIR
    n_slabs = seq // slab
    pv = pv_ref[...]
    prow = lambda i: pv[i:i + 1, :]
    w2 = lw_ref[0:DECAY_RANK, :]
    a2 = lw_ref[DECAY_RANK:DECAY_RANK + ICLR_RANK, :]
    g2 = lw_ref[DECAY_RANK + ICLR_RANK:, :]

    iota = lambda shape, dim: lax.broadcasted_iota(jnp.int32, shape, dim)
    srow, scol = iota((slab, slab), 0), iota((slab, slab), 1)
    tri_incl = ((srow // CHUNK == scol // CHUNK) & (srow >= scol)).astype(BF16)
    first_row = iota((slab, 1), 0) == 0
    head0 = iota((1, PAIR), 1) < HEAD_DIM
    head_c = (head0, jnp.logical_not(head0))
    row4 = iota((4 * CHUNK, PAIR), 0) % (2 * CHUNK)
    col4 = iota((4 * CHUNK, PAIR), 1) % CHUNK
    keep4 = jnp.where(row4 < CHUNK, row4, row4 - CHUNK + 1) > col4
    plane = iota((1, PAIR), 1) < CHUNK
    eye_p = iota((PAIR, PAIR), 0) == iota((PAIR, PAIR), 1)
    same_head = (iota((PAIR, PAIR), 0) // HEAD_DIM) == (iota((PAIR, PAIR), 1) // HEAD_DIM)
    zeros_b = jnp.zeros((CHUNK, PAIR), BF16)

    def head_sum(x):
        outs = []
        for p in range(x.shape[1] // PAIR):
            xp = x[:, p * PAIR:(p + 1) * PAIR]
            s0 = jnp.sum(jnp.where(head0, xp, 0.0), axis=-1, keepdims=True)
            s1 = jnp.sum(jnp.where(head0, 0.0, xp), axis=-1, keepdims=True)
            outs.append(jnp.where(head0, s0, s1))
        return outs[0] if len(outs) == 1 else jnp.concatenate(outs, axis=1)

    def with_shift(ref, i):
        t0 = pl.multiple_of(i * slab, slab)
        cur = ref[pl.ds(t0, slab), :]
        last = ref[pl.ds(jnp.maximum(t0 - 1, 0), 1), :] * jnp.where(i > 0, 1.0, 0.0).astype(F32)
        return cur, jnp.where(first_row, last, pltpu.roll(cur, 1, 0))

    def lowrank_stage(i):
        la, _ = with_shift(la_ref, i)
        _, lb_sh = with_shift(lb_ref, i)
        low = la + lb_sh
        w_pre = prow(_W0) + _dot(jnp.tanh(low[:, :DECAY_RANK]), w2)
        log_decay = -DECAY_SCALE * jax.nn.sigmoid(w_pre)
        ld_hi = log_decay.astype(BF16)
        ld_lo = (log_decay - ld_hi.astype(F32)).astype(BF16)
        ld_s[...] = log_decay
        cum_s[...] = (jnp.dot(tri_incl, ld_hi, preferred_element_type=F32)
                      + jnp.dot(tri_incl, ld_lo, preferred_element_type=F32))
        iclr_s[...] = jax.nn.sigmoid(prow(_A0) + _dot(low[:, DECAY_RANK:DECAY_RANK + ICLR_RANK], a2))
        gate_s[...] = _dot(jax.nn.sigmoid(low[:, DECAY_RANK + ICLR_RANK:]), g2)

    def slab_body(i, states):
        t0 = pl.multiple_of(i * slab, slab)

        def mixed(ref, mu):
            cur, sh = with_shift(ref, i)
            return cur + (sh - cur) * mu

        r = mixed(r_ref, prow(_MU_R))
        k = mixed(k_ref, prow(_MU_K))
        v = mixed(v_ref, prow(_MU_V))
        log_decay, cum, iclr = ld_s[...], cum_s[...], iclr_s[...]

        kk = k * prow(_K_K)
        kk = kk * lax.rsqrt(jnp.maximum(head_sum(kk * kk), 1e-24))
        k = k * (1.0 + (iclr - 1.0) * prow(_K_A))
        b_vec = kk * iclr

        e_neg = jnp.exp(-cum)
        a_t = -kk * jnp.exp(cum - log_decay)
        r_t = r * jnp.exp(cum)
        b_t = b_vec * e_neg
        k_t = k * e_neg
        bonus = head_sum(r * k * prow(_R_K)) * v

        units = [(j, p) for j in range(SLAB_CHUNKS) for p in range(n_pairs)]
        heads = range(2)
        blk = lambda x, u: x[u[0] * CHUNK:(u[0] + 1) * CHUNK, u[1] * PAIR:(u[1] + 1) * PAIR]
        cum_end = {u: cum[(u[0] + 1) * CHUNK - 1:(u[0] + 1) * CHUNK, u[1] * PAIR:(u[1] + 1) * PAIR]
                   for u in units}
        at = {u: [jnp.where(head_c[h], blk(a_t, u), 0.0) for h in heads] for u in units}
        rtb = {u: blk(r_t, u).astype(BF16) for u in units}
        vb = {u: blk(v, u).astype(BF16) for u in units}
        rt = {u: [jnp.where(head_c[h], rtb[u], zeros_b) for h in heads] for u in units}
        vh = {u: [jnp.where(head_c[h], vb[u], zeros_b) for h in heads] for u in units}
        xab = {u: [at[u][h].astype(BF16) for h in heads] for u in units}
        big = {u: jnp.where(keep4,
                            _dot_nt(jnp.concatenate([xab[u][0], rt[u][0], xab[u][1], rt[u][1]], axis=0),
                                    jnp.concatenate([blk(b_t, u), blk(k_t, u)], axis=0)), 0.0)
               for u in units}
        bot = {u: [big[u][(2 * h + 1) * CHUNK:(2 * h + 2) * CHUNK] for h in heads] for u in units}
        px = {u: [big[u][2 * h * CHUNK:(2 * h + 1) * CHUNK] for h in heads] for u in units}
        xa = at
        pxb = {u: [px[u][h].astype(BF16) for h in heads] for u in units}
        for _ in range(6):
            res = {u: [_dot(pxb[u][h][:, :CHUNK], jnp.concatenate([pxb[u][h], xab[u][h]], axis=1))
                       for h in heads] for u in units}
            px = {u: [res[u][h][:, :PAIR] + jnp.where(plane, 0.0, px[u][h]) for h in heads] for u in units}
            xa = {u: [res[u][h][:, PAIR:] + xa[u][h] for h in heads] for u in units}
            pxb = {u: [px[u][h].astype(BF16) for h in heads] for u in units}
            xab = {u: [xa[u][h].astype(BF16) for h in heads] for u in units}
        u_v = {u: [_dot(pxb[u][h], jnp.concatenate([zeros_b, vh[u][h]], axis=0)) for h in heads]
               for u in units}
        ry = {u: [_dot(bot[u][h],
                       jnp.concatenate([jnp.concatenate([xab[u][h], u_v[u][h].astype(BF16)], axis=1),
                                        jnp.concatenate([zeros_b, vh[u][h]], axis=1)], axis=0))
                  for h in heads] for u in units}
        mn = {}
        for u in units:
            e_end = jnp.exp(cum_end[u] - blk(cum, u))
            bk_h = jnp.concatenate([blk(b_vec, u) * e_end, blk(k, u) * e_end], axis=0)
            apuv = jnp.concatenate(
                [jnp.concatenate([xa[u][0] + xa[u][1], u_v[u][0] + u_v[u][1]], axis=1).astype(BF16),
                 jnp.concatenate([zeros_b, vb[u]], axis=1)], axis=0)
            mn[u] = _dot_tn(bk_h, apuv)
        states = list(states)
        gate = gate_s[...]
        for u in units:
            j, p = u
            m = mn[u][:, :PAIR] + jnp.where(eye_p, jnp.exp(cum_end[u]), 0.0)
            r_p = blk(r_t, u) + ry[u][0][:, :PAIR] + ry[u][1][:, :PAIR]
            out = _dot(jnp.concatenate([m, r_p], axis=0), states[p])
            states[p] = jnp.where(same_head, out[:PAIR] + mn[u][:, PAIR:], 0.0)
            y = out[PAIR:] + ry[u][0][:, PAIR:] + ry[u][1][:, PAIR:]
            yc = y - head_sum(y) * (1.0 / HEAD_DIM)
            var = head_sum(yc * yc) * (1.0 / HEAD_DIM)
            lanes = slice(p * PAIR, (p + 1) * PAIR)
            y = yc * lax.rsqrt(var + GN_EPS) * prow(_LNX_G)[:, lanes] + prow(_LNX_B)[:, lanes]
            o_ref[pl.ds(t0 + j * CHUNK, CHUNK), lanes] = ((y + blk(bonus, u)) * blk(gate, u)).astype(o_ref.dtype)
        lowrank_stage(jnp.minimum(i + 1, n_slabs - 1))
        return tuple(states)

    zero_state = jnp.zeros((PAIR, PAIR), F32)
    lowrank_stage(0)
    lax.fori_loop(0, n_slabs, slab_body, (zero_state,) * n_pairs)


def _rwkv(proj3, layer, pvec, lowrank_w):
    bsz, seq, _ = proj3.shape
    col = lambda blk: pl.BlockSpec((None, seq, RWKV_WIDTH), lambda b, blk=blk: (b, 0, blk))
    low = lambda blk: pl.BlockSpec((None, seq, LOW_WIDTH), lambda b, blk=blk: (b, 0, blk))
    full = lambda a: _layer_spec(a, layer)
    return pl.pallas_call(
        _rwkv_kernel,
        grid=(bsz,),
        in_specs=[col(0), col(1), col(2),
                  low(3 * RWKV_WIDTH // LOW_WIDTH), low(3 * RWKV_WIDTH // LOW_WIDTH + 1),
                  full(pvec), full(lowrank_w)],
        out_specs=pl.BlockSpec((None, seq, RWKV_WIDTH), lambda b: (b, 0, 0)),
        out_shape=jax.ShapeDtypeStruct((bsz, seq, RWKV_WIDTH), BF16),
        scratch_shapes=[pltpu.VMEM((SLAB_CHUNKS * CHUNK, RWKV_WIDTH), F32)] * 4,
        compiler_params=pltpu.CompilerParams(
            dimension_semantics=("parallel",), vmem_limit_bytes=VMEM_LIMIT),
        name="rwkv",
    )(proj3, proj3, proj3, proj3, proj3, pvec, lowrank_w)


def _outffn_kernel(x_ref, yr_ref, mixc_ref, wo_ref, wg_ref, wu_ref, wd_ref, g_ref, wn32_ref, o_ref, wn_ref):
    wn_ref[...] = wn32_ref[...].astype(BF16)
    half = x_ref.shape[0] // FFN_PARTS
    halves = [slice(n * half, (n + 1) * half) for n in range(FFN_PARTS)]
    mix = [jnp.dot(yr_ref[s, :], wo_ref[...], preferred_element_type=F32) + mixc_ref[s, :] for s in halves]
    x = [x_ref[s, :] + _rms(m, g_ref[1:2, :]) for s, m in zip(halves, mix)]
    h2 = [_rms(xs, g_ref[2:3, :]).astype(BF16) for xs in x]
    act = []
    for h in h2:
        gate = jnp.dot(h, wg_ref[...], preferred_element_type=F32)
        up = jnp.dot(h, wu_ref[...], preferred_element_type=F32)
        act.append((gate * jax.nn.sigmoid(gate) * up).astype(BF16))
    f = [jnp.dot(a, wd_ref[...], preferred_element_type=F32) for a in act]
    for s, xs, fs in zip(halves, x, f):
        o_ref[s, :] = xs + _rms(fs, g_ref[3:4, :])


def _outffn(x2, y_rwkv, mix_conv, layer, wo, wg, wu, wd, gains, w_in, next_layer):
    rows = x2.shape[0]
    n_tiles = rows // ROW_TILE
    in_rows = D_MODEL // n_tiles
    assert in_rows * n_tiles == D_MODEL and in_rows % BF16_ROWS == 0
    tile = lambda a: pl.BlockSpec((ROW_TILE, a.shape[1]), lambda i: (i, 0))
    whole = lambda a: pl.BlockSpec(a.shape, lambda i: (0, 0), pipeline_mode=pl.Buffered(1))
    return pl.pallas_call(
        _outffn_kernel,
        grid=(n_tiles,),
        in_specs=[tile(x2), tile(y_rwkv), tile(mix_conv),
                  _layer_spec(wo, layer, (RWKV_WIDTH, D_MODEL), pipeline_mode=pl.Buffered(1)),
                  whole(wg), whole(wu), whole(wd),
                  _layer_spec(gains, layer, pipeline_mode=pl.Buffered(1)),
                  pl.BlockSpec((None, in_rows, w_in.shape[2]), lambda i: (next_layer, i, 0))],
        out_specs=[tile(x2), pl.BlockSpec((in_rows, w_in.shape[2]), lambda i: (i, 0))],
        out_shape=[jax.ShapeDtypeStruct(x2.shape, F32),
                   jax.ShapeDtypeStruct(w_in.shape[1:], BF16)],
        compiler_params=pltpu.CompilerParams(
            dimension_semantics=("parallel",), vmem_limit_bytes=VMEM_LIMIT),
        name="outffn",
    )(x2, y_rwkv, mix_conv, wo, wg, wu, wd, gains, w_in)


def _fold_low_rank(mu, w):
    return (1.0 - mu)[..., None] * w, mu[..., None] * w


def kernel(x, w_in, mu_rkv, mu_wag, w0, w1, w2, a0, a1, a2, g1, g2, k_k, k_a, r_k, lnx_g, lnx_b,
           sc_conv_w, cf_conv_w, cf_conv_b, cf_ln_g, cf_ln_b, w_o, w_gate, w_up, w_down,
           pre_mix_g, post_mix_g, pre_ffn_g, post_ffn_g):
    bsz, seq, d = x.shape
    depth = w_in.shape[0]
    w1a, w1b = _fold_low_rank(mu_wag[:, 0], w1)
    a1a, a1b = _fold_low_rank(mu_wag[:, 1], a1)
    g1a, g1b = _fold_low_rank(mu_wag[:, 2], g1)
    w_low = jnp.concatenate([w1a, a1a, g1a, w1b, a1b, g1b], axis=2).astype(BF16)
    w_inb = w_in[0].astype(BF16)
    pvec = jnp.concatenate([
        mu_rkv.reshape(depth, 3, RWKV_WIDTH),
        jnp.stack([w0, a0, k_k, k_a, r_k.reshape(depth, RWKV_WIDTH), lnx_g, lnx_b], axis=1),
        jnp.zeros((depth, _PVEC_ROWS - 10, RWKV_WIDTH), F32)], axis=1)
    lowrank_w = jnp.concatenate([w2, a2, g2], axis=1).astype(BF16)
    cf_p = jnp.stack([cf_conv_b, cf_ln_g, cf_ln_b], axis=1)
    gains = jnp.stack([pre_mix_g, post_mix_g, pre_ffn_g, post_ffn_g], axis=1)
    wo = w_o.astype(BF16)

    x2 = x.reshape(bsz * seq, d)
    for l in range(depth):
        proj, mix_conv, wg, wu, wd = _inproj_conv(x2, l, gains, w_inb, w_low, wo, sc_conv_w, cf_conv_w, cf_p,
                                                  w_gate, w_up, w_down, seq)
        proj3 = proj.reshape(bsz, seq, RWKV_COLS)
        y_rwkv = _rwkv(proj3, l, pvec, lowrank_w).reshape(bsz * seq, RWKV_WIDTH)
        x2, w_inb = _outffn(x2, y_rwkv, mix_conv, l, wo, wg, wu, wd, gains, w_in, min(l + 1, depth - 1))
    return x2.reshape(bsz, seq, d)
```

```python
import functools
import math

import jax
import jax.numpy as jnp
from jax import lax
from jax.experimental import pallas as pl
from jax.experimental.pallas import tpu as pltpu

D_MODEL = 1024
HEAD_DIM = 64
RWKV_WIDTH = 512
SC_WIDTH = 256
CF_WIDTH = 256
SC_KERNEL = 3
CF_KERNEL = 31
DECAY_RANK = 64
ICLR_RANK = 64
GATE_RANK = 128
LOW_WIDTH = DECAY_RANK + ICLR_RANK + GATE_RANK
IN_WIDTH = 3 * RWKV_WIDTH + 3 * SC_WIDTH + 2 * CF_WIDTH
PROJ_WIDTH = IN_WIDTH + 2 * LOW_WIDTH
NORM_EPS = 1e-6
LN_EPS = 1e-5
GN_EPS = 64e-5
MIX_WIDTH = RWKV_WIDTH + SC_WIDTH + CF_WIDTH
DECAY_SCALE = math.exp(-0.5)

CHUNK = 64
PAIR = 2 * HEAD_DIM
SLAB_CHUNKS = 2
ROW_TILE = 512
DOT_COLS = 256
CONV_COLS = 3 * SC_WIDTH + 2 * CF_WIDTH
RWKV_COLS = PROJ_WIDTH - CONV_COLS
CONV_ROWS = 64
CONV_BLOCKS_PER_DOT = 1
CONV_FIRST_DOT = 3
OUT_ROWS = 128
FFN_PARTS = 2
CF_HALO = 32
SC_HALO = 8
SUBLANES = 8
BF16_ROWS = 16
VMEM_LIMIT = 56 * 1024 * 1024

F32 = jnp.float32
BF16 = jnp.bfloat16


def _dot(a, b):
    return jnp.dot(a.astype(BF16), b.astype(BF16), preferred_element_type=F32)


def _dot_nt(a, b):
    return lax.dot_general(a.astype(BF16), b.astype(BF16), (((1,), (1,)), ((), ())),
                           preferred_element_type=F32)


def _dot_tn(a, b):
    return lax.dot_general(a.astype(BF16), b.astype(BF16), (((0,), (0,)), ((), ())),
                           preferred_element_type=F32)


def _rms(x, g):
    return x * lax.rsqrt(jnp.mean(x * x, axis=-1, keepdims=True) + NORM_EPS) * g


def _inproj_conv_kernel(x_ref, g_ref, w_ref, wl_ref, woc_ref, scw_ref, cfw_ref, cfp_ref,
                        wg32_ref, wu32_ref, wd32_ref, proj_ref, mixc_ref, wg_ref, wu_ref, wd_ref,
                        held, gb_buf, sc_buf, cf_buf, cf_sh, y_buf, *, tiles_per_seq):
    i = pl.program_id(0)
    tile = x_ref.shape[0]
    wg_ref[...] = wg32_ref[...].astype(BF16)
    wu_ref[...] = wu32_ref[...].astype(BF16)
    wd_ref[...] = wd32_ref[...].astype(BF16)

    @pl.when(i == 0)
    def _():
        held[...] = jnp.zeros(held.shape, F32)
        sc_buf[0:SC_HALO, :] = jnp.zeros((SC_HALO, SC_WIDTH), F32)
        cf_buf[0:CF_HALO, :] = jnp.zeros((CF_HALO, CF_WIDTH), F32)

    col = lambda n: slice(n * SC_WIDTH, (n + 1) * SC_WIDTH)
    gb_buf[...] = held[:, col(0)]
    sc_buf[SC_HALO:, :] = held[:, col(1)] * held[:, col(2)]
    cf_buf[CF_HALO:, :] = held[:, col(3)] * jax.nn.sigmoid(held[:, col(4)])

    n_sh = cf_sh.shape[1]
    for res in range(1, SUBLANES):
        cf_sh[res - 1] = cf_buf[pl.ds(res, n_sh), :]

    h = _rms(x_ref[...], g_ref[0:1, :]).astype(BF16)

    def project(j):
        c0 = j * DOT_COLS
        cols = slice(c0, c0 + DOT_COLS)
        if c0 < 3 * RWKV_WIDTH:
            proj_ref[:, cols] = jnp.dot(h, w_ref[:, cols], preferred_element_type=F32)
        elif c0 < RWKV_COLS:
            proj_ref[:, cols] = jnp.dot(h, wl_ref[:, c0 - 3 * RWKV_WIDTH:c0 - 3 * RWKV_WIDTH + DOT_COLS],
                                        preferred_element_type=F32)
        else:
            src = c0 - 2 * LOW_WIDTH
            held[:, c0 - RWKV_COLS:c0 - RWKV_COLS + DOT_COLS] = jnp.dot(
                h, w_ref[:, src:src + DOT_COLS], preferred_element_type=F32)

    def conv_rows(r0):
        acc = jnp.zeros((CONV_ROWS, SC_WIDTH), F32)
        for j in range(SC_KERNEL):
            acc = acc + scw_ref[j:j + 1, :] * sc_buf[pl.ds(r0 + SC_HALO - (SC_KERNEL - 1) + j, CONV_ROWS), :]
        y_buf[r0:r0 + CONV_ROWS, 0:SC_WIDTH] = (gb_buf[r0:r0 + CONV_ROWS, :] * acc).astype(y_buf.dtype)
        acc = jnp.zeros((CONV_ROWS, CF_WIDTH), F32) + cfp_ref[0:1, :]
        for j in range(CF_KERNEL):
            start = r0 + CF_HALO - (CF_KERNEL - 1) + j
            res = start % SUBLANES
            if res == 0:
                taps = cf_buf[pl.ds(start, CONV_ROWS), :]
            else:
                taps = cf_sh[res - 1, pl.ds(start - res, CONV_ROWS), :]
            acc = acc + cfw_ref[j:j + 1, :] * taps
        mu = jnp.mean(acc, axis=-1, keepdims=True)
        cen = acc - mu
        var = jnp.mean(cen * cen, axis=-1, keepdims=True)
        z = cen * lax.rsqrt(var + LN_EPS) * cfp_ref[1:2, :] + cfp_ref[2:3, :]
        y_buf[r0:r0 + CONV_ROWS, SC_WIDTH:] = (z * jax.nn.sigmoid(z)).astype(y_buf.dtype)

    n_conv = tile // CONV_ROWS
    done = 0
    for j in range(PROJ_WIDTH // DOT_COLS):
        project(j)
        upto = min(n_conv, done + CONV_BLOCKS_PER_DOT) if j >= CONV_FIRST_DOT else done
        for c in range(done, upto):
            conv_rows(c * CONV_ROWS)
            r1 = (c + 1) * CONV_ROWS
            if r1 % OUT_ROWS == 0:
                rows = slice(r1 - OUT_ROWS, r1)
                mixc_ref[rows, :] = jnp.dot(y_buf[rows, :], woc_ref[...], preferred_element_type=F32)
        done = upto
    assert done == n_conv

    keep = jnp.where(i % tiles_per_seq != 0, 1.0, 0.0).astype(F32)
    sc_buf[0:SC_HALO, :] = sc_buf[tile:tile + SC_HALO, :] * keep
    cf_buf[0:CF_HALO, :] = cf_buf[tile:tile + CF_HALO, :] * keep


def _layer_spec(a, layer, block=None, index=(0, 0), **kwargs):
    block = a.shape[1:] if block is None else block
    return pl.BlockSpec((None,) + tuple(block), lambda *_: (layer,) + tuple(index), **kwargs)


def _inproj_conv(x2, layer, gains, w, w_low, wo, sc_w, cf_w, cf_p, w_gate, w_up, w_down, seq):
    rows = x2.shape[0]
    n_tiles = rows // ROW_TILE
    ffn = w_gate.shape[2]
    full = lambda a: _layer_spec(a, layer)
    cur = lambda i: (jnp.minimum(i, n_tiles - 1), 0)
    up_rows = D_MODEL // n_tiles
    assert up_rows * n_tiles == D_MODEL and up_rows % BF16_ROWS == 0 and ffn % DOT_COLS == 0
    down_blocks = ffn // DOT_COLS
    assert down_blocks <= n_tiles
    up_blk = lambda i: jnp.minimum(i, n_tiles - 1)
    down_blk = lambda i: jnp.minimum(i, down_blocks - 1)
    return pl.pallas_call(
        functools.partial(_inproj_conv_kernel, tiles_per_seq=seq // ROW_TILE),
        grid=(n_tiles + 1,),
        in_specs=[
            pl.BlockSpec((ROW_TILE, D_MODEL), cur),
            full(gains),
            pl.BlockSpec(w.shape, lambda i: (0, 0), pipeline_mode=pl.Buffered(1)),
            _layer_spec(w_low, layer, pipeline_mode=pl.Buffered(1)),
            _layer_spec(wo, layer, (MIX_WIDTH - RWKV_WIDTH, D_MODEL), (1, 0)),
            full(sc_w), full(cf_w), full(cf_p),
            pl.BlockSpec((None, up_rows, ffn), lambda i: (layer, up_blk(i), 0)),
            pl.BlockSpec((None, up_rows, ffn), lambda i: (layer, up_blk(i), 0)),
            pl.BlockSpec((None, DOT_COLS, D_MODEL), lambda i: (layer, down_blk(i), 0)),
        ],
        out_specs=[pl.BlockSpec((ROW_TILE, RWKV_COLS), cur),
                   pl.BlockSpec((ROW_TILE, D_MODEL), lambda i: (jnp.maximum(i - 1, 0), 0)),
                   pl.BlockSpec((up_rows, ffn), lambda i: (up_blk(i), 0)),
                   pl.BlockSpec((up_rows, ffn), lambda i: (up_blk(i), 0)),
                   pl.BlockSpec((DOT_COLS, D_MODEL), lambda i: (down_blk(i), 0))],
        out_shape=[jax.ShapeDtypeStruct((rows, RWKV_COLS), F32),
                   jax.ShapeDtypeStruct((rows, D_MODEL), F32),
                   jax.ShapeDtypeStruct((D_MODEL, ffn), BF16),
                   jax.ShapeDtypeStruct((D_MODEL, ffn), BF16),
                   jax.ShapeDtypeStruct((ffn, D_MODEL), BF16)],
        scratch_shapes=[pltpu.VMEM((ROW_TILE, CONV_COLS), F32),
                        pltpu.VMEM((ROW_TILE, SC_WIDTH), F32),
                        pltpu.VMEM((SC_HALO + ROW_TILE, SC_WIDTH), F32),
                        pltpu.VMEM((CF_HALO + ROW_TILE, CF_WIDTH), F32),
                        pltpu.VMEM((SUBLANES - 1, CF_HALO + ROW_TILE - SUBLANES, CF_WIDTH), F32),
                        pltpu.VMEM((ROW_TILE, SC_WIDTH + CF_WIDTH), BF16)],
        compiler_params=pltpu.CompilerParams(
            dimension_semantics=("arbitrary",), vmem_limit_bytes=VMEM_LIMIT),
        name="inproj_conv",
    )(x2, gains, w, w_low, wo, sc_w, cf_w, cf_p, w_gate, w_up, w_down)


_MU_R, _MU_K, _MU_V, _W0, _A0, _K_K, _K_A, _R_K, _LNX_G, _LNX_B = range(10)
_PVEC_ROWS = 16


def _rwkv_kernel(r_ref, k_ref, v_ref, la_ref, lb_ref, pv_ref, lw_ref, o_ref, ld_s, cum_s, iclr_s, gate_s):
    seq = r_ref.shape[0]
    slab = SLAB_CHUNKS * CHUNK
    n_pairs = RWKV_WIDTH // PAIR
    n_slabs = seq // slab
    pv = pv_ref[...]
    prow = lambda i: pv[i:i + 1, :]
    w2 = lw_ref[0:DECAY_RANK, :]
    a2 = lw_ref[DECAY_RANK:DECAY_RANK + ICLR_RANK, :]
    g2 = lw_ref[DECAY_RANK + ICLR_RANK:, :]

    iota = lambda shape, dim: lax.broadcasted_iota(jnp.int32, shape, dim)
    srow, scol = iota((slab, slab), 0), iota((slab, slab), 1)
    tri_incl = ((srow // CHUNK == scol // CHUNK) & (srow >= scol)).astype(BF16)
    first_row = iota((slab, 1), 0) == 0
    head0 = iota((1, PAIR), 1) < HEAD_DIM
    head_c = (head0, jnp.logical_not(head0))
    row4 = iota((4 * CHUNK, PAIR), 0) % (2 * CHUNK)
    col4 = iota((4 * CHUNK, PAIR), 1) % CHUNK
    keep4 = jnp.where(row4 < CHUNK, row4, row4 - CHUNK + 1) > col4
    plane = iota((1, PAIR), 1) < CHUNK
    eye_p = iota((PAIR, PAIR), 0) == iota((PAIR, PAIR), 1)
    same_head = (iota((PAIR, PAIR), 0) // HEAD_DIM) == (iota((PAIR, PAIR), 1) // HEAD_DIM)
    zeros_b = jnp.zeros((CHUNK, PAIR), BF16)

    def head_sum(x):
        outs = []
        for p in range(x.shape[1] // PAIR):
            xp = x[:, p * PAIR:(p + 1) * PAIR]
            s0 = jnp.sum(jnp.where(head0, xp, 0.0), axis=-1, keepdims=True)
            s1 = jnp.sum(jnp.where(head0, 0.0, xp), axis=-1, keepdims=True)
            outs.append(jnp.where(head0, s0, s1))
        return outs[0] if len(outs) == 1 else jnp.concatenate(outs, axis=1)

    def with_shift(ref, i):
        t0 = pl.multiple_of(i * slab, slab)
        cur = ref[pl.ds(t0, slab), :]
        last = ref[pl.ds(jnp.maximum(t0 - 1, 0), 1), :] * jnp.where(i > 0, 1.0, 0.0).astype(F32)
        return cur, jnp.where(first_row, last, pltpu.roll(cur, 1, 0))

    def lowrank_dots(i):
        la, _ = with_shift(la_ref, i)
        _, lb_sh = with_shift(lb_ref, i)
        low = la + lb_sh
        w_pre = prow(_W0) + _dot(jnp.tanh(low[:, :DECAY_RANK]), w2)
        log_decay = -DECAY_SCALE * jax.nn.sigmoid(w_pre)
        iclr = jax.nn.sigmoid(prow(_A0) + _dot(low[:, DECAY_RANK:DECAY_RANK + ICLR_RANK], a2))
        gate = _dot(jax.nn.sigmoid(low[:, DECAY_RANK + ICLR_RANK:]), g2)
        return log_decay, iclr, gate

    def hand_over(log_decay, iclr, gate):
        ld_hi = log_decay.astype(BF16)
        ld_lo = (log_decay - ld_hi.astype(F32)).astype(BF16)
        ld_s[...] = log_decay
        cum_s[...] = (jnp.dot(tri_incl, ld_hi, preferred_element_type=F32)
                      + jnp.dot(tri_incl, ld_lo, preferred_element_type=F32))
        iclr_s[...] = iclr
        gate_s[...] = gate

    def slab_body(i, states):
        t0 = pl.multiple_of(i * slab, slab)

        def mixed(ref, mu):
            cur, sh = with_shift(ref, i)
            return cur + (sh - cur) * mu

        r = mixed(r_ref, prow(_MU_R))
        k = mixed(k_ref, prow(_MU_K))
        v = mixed(v_ref, prow(_MU_V))
        log_decay, cum, iclr = ld_s[...], cum_s[...], iclr_s[...]

        kk = k * prow(_K_K)
        kk = kk * lax.rsqrt(jnp.maximum(head_sum(kk * kk), 1e-24))
        k = k * (1.0 + (iclr - 1.0) * prow(_K_A))
        b_vec = kk * iclr

        e_neg = jnp.exp(-cum)
        a_t = -kk * jnp.exp(cum - log_decay)
        r_t = r * jnp.exp(cum)
        b_t = b_vec * e_neg
        k_t = k * e_neg
        bonus = head_sum(r * k * prow(_R_K)) * v

        units = [(j, p) for j in range(SLAB_CHUNKS) for p in range(n_pairs)]
        heads = range(2)
        blk = lambda x, u: x[u[0] * CHUNK:(u[0] + 1) * CHUNK, u[1] * PAIR:(u[1] + 1) * PAIR]
        cum_end = {u: cum[(u[0] + 1) * CHUNK - 1:(u[0] + 1) * CHUNK, u[1] * PAIR:(u[1] + 1) * PAIR]
                   for u in units}
        at = {u: [jnp.where(head_c[h], blk(a_t, u), 0.0) for h in heads] for u in units}
        rtb = {u: blk(r_t, u).astype(BF16) for u in units}
        vb = {u: blk(v, u).astype(BF16) for u in units}
        rt = {u: [jnp.where(head_c[h], rtb[u], zeros_b) for h in heads] for u in units}
        vh = {u: [jnp.where(head_c[h], vb[u], zeros_b) for h in heads] for u in units}
        xab = {u: [at[u][h].astype(BF16) for h in heads] for u in units}
        big = {u: jnp.where(keep4,
                            _dot_nt(jnp.concatenate([xab[u][0], rt[u][0], xab[u][1], rt[u][1]], axis=0),
                                    jnp.concatenate([blk(b_t, u), blk(k_t, u)], axis=0)), 0.0)
               for u in units}
        bot = {u: [big[u][(2 * h + 1) * CHUNK:(2 * h + 2) * CHUNK] for h in heads] for u in units}
        px = {u: [big[u][2 * h * CHUNK:(2 * h + 1) * CHUNK] for h in heads] for u in units}
        xa = at
        pxb = {u: [px[u][h].astype(BF16) for h in heads] for u in units}
        for _ in range(6):
            res = {u: [_dot(pxb[u][h][:, :CHUNK], jnp.concatenate([pxb[u][h], xab[u][h]], axis=1))
                       for h in heads] for u in units}
            px = {u: [res[u][h][:, :PAIR] + jnp.where(plane, 0.0, px[u][h]) for h in heads] for u in units}
            xa = {u: [res[u][h][:, PAIR:] + xa[u][h] for h in heads] for u in units}
            pxb = {u: [px[u][h].astype(BF16) for h in heads] for u in units}
            xab = {u: [xa[u][h].astype(BF16) for h in heads] for u in units}
        u_v = {u: [_dot(pxb[u][h], jnp.concatenate([zeros_b, vh[u][h]], axis=0)) for h in heads]
               for u in units}
        ry = {u: [_dot(bot[u][h],
                       jnp.concatenate([jnp.concatenate([xab[u][h], u_v[u][h].astype(BF16)], axis=1),
                                        jnp.concatenate([zeros_b, vh[u][h]], axis=1)], axis=0))
                  for h in heads] for u in units}
        mn = {}
        for u in units:
            e_end = jnp.exp(cum_end[u] - blk(cum, u))
            bk_h = jnp.concatenate([blk(b_vec, u) * e_end, blk(k, u) * e_end], axis=0)
            apuv = jnp.concatenate(
                [jnp.concatenate([xa[u][0] + xa[u][1], u_v[u][0] + u_v[u][1]], axis=1).astype(BF16),
                 jnp.concatenate([zeros_b, vb[u]], axis=1)], axis=0)
            mn[u] = _dot_tn(bk_h, apuv)
        states = list(states)
        gate = gate_s[...]
        next_low = None
        for u in units:
            j, p = u
            if j == 1 and next_low is None:
                next_low = lowrank_dots(jnp.minimum(i + 1, n_slabs - 1))
            m = mn[u][:, :PAIR] + jnp.where(eye_p, jnp.exp(cum_end[u]), 0.0)
            r_p = blk(r_t, u) + ry[u][0][:, :PAIR] + ry[u][1][:, :PAIR]
            out = _dot(jnp.concatenate([m, r_p], axis=0), states[p])
            states[p] = jnp.where(same_head, out[:PAIR] + mn[u][:, PAIR:], 0.0)
            y = out[PAIR:] + ry[u][0][:, PAIR:] + ry[u][1][:, PAIR:]
            yc = y - head_sum(y) * (1.0 / HEAD_DIM)
            var = head_sum(yc * yc) * (1.0 / HEAD_DIM)
            lanes = slice(p * PAIR, (p + 1) * PAIR)
            y = yc * lax.rsqrt(var + GN_EPS) * prow(_LNX_G)[:, lanes] + prow(_LNX_B)[:, lanes]
            o_ref[pl.ds(t0 + j * CHUNK, CHUNK), lanes] = ((y + blk(bonus, u)) * blk(gate, u)).astype(o_ref.dtype)
        hand_over(*next_low)
        return tuple(states)

    zero_state = jnp.zeros((PAIR, PAIR), F32)
    hand_over(*lowrank_dots(0))
    lax.fori_loop(0, n_slabs, slab_body, (zero_state,) * n_pairs)


def _rwkv(proj3, layer, pvec, lowrank_w):
    bsz, seq, _ = proj3.shape
    col = lambda blk: pl.BlockSpec((None, seq, RWKV_WIDTH), lambda b, blk=blk: (b, 0, blk))
    low = lambda blk: pl.BlockSpec((None, seq, LOW_WIDTH), lambda b, blk=blk: (b, 0, blk))
    full = lambda a: _layer_spec(a, layer)
    return pl.pallas_call(
        _rwkv_kernel,
        grid=(bsz,),
        in_specs=[col(0), col(1), col(2),
                  low(3 * RWKV_WIDTH // LOW_WIDTH), low(3 * RWKV_WIDTH // LOW_WIDTH + 1),
                  full(pvec), full(lowrank_w)],
        out_specs=pl.BlockSpec((None, seq, RWKV_WIDTH), lambda b: (b, 0, 0)),
        out_shape=jax.ShapeDtypeStruct((bsz, seq, RWKV_WIDTH), BF16),
        scratch_shapes=[pltpu.VMEM((SLAB_CHUNKS * CHUNK, RWKV_WIDTH), F32)] * 4,
        compiler_params=pltpu.CompilerParams(
            dimension_semantics=("parallel",), vmem_limit_bytes=VMEM_LIMIT),
        name="rwkv",
    )(proj3, proj3, proj3, proj3, proj3, pvec, lowrank_w)


def _outffn_kernel(x_ref, yr_ref, mixc_ref, wo_ref, wg_ref, wu_ref, wd_ref, g_ref, wn32_ref, o_ref, wn_ref):
    wn_ref[...] = wn32_ref[...].astype(BF16)
    half = x_ref.shape[0] // FFN_PARTS
    halves = [slice(n * half, (n + 1) * half) for n in range(FFN_PARTS)]
    mix = [jnp.dot(yr_ref[s, :], wo_ref[...], preferred_element_type=F32) + mixc_ref[s, :] for s in halves]
    x = [x_ref[s, :] + _rms(m, g_ref[1:2, :]) for s, m in zip(halves, mix)]
    h2 = [_rms(xs, g_ref[2:3, :]).astype(BF16) for xs in x]
    act = []
    for h in h2:
        gate = jnp.dot(h, wg_ref[...], preferred_element_type=F32)
        up = jnp.dot(h, wu_ref[...], preferred_element_type=F32)
        act.append((gate * jax.nn.sigmoid(gate) * up).astype(BF16))
    f = [jnp.dot(a, wd_ref[...], preferred_element_type=F32) for a in act]
    for s, xs, fs in zip(halves, x, f):
        o_ref[s, :] = xs + _rms(fs, g_ref[3:4, :])


def _outffn(x2, y_rwkv, mix_conv, layer, wo, wg, wu, wd, gains, w_in, next_layer):
    rows = x2.shape[0]
    n_tiles = rows // ROW_TILE
    in_rows = D_MODEL // n_tiles
    assert in_rows * n_tiles == D_MODEL and in_rows % BF16_ROWS == 0
    tile = lambda a: pl.BlockSpec((ROW_TILE, a.shape[1]), lambda i: (i, 0))
    whole = lambda a: pl.BlockSpec(a.shape, lambda i: (0, 0), pipeline_mode=pl.Buffered(1))
    return pl.pallas_call(
        _outffn_kernel,
        grid=(n_tiles,),
        in_specs=[tile(x2), tile(y_rwkv), tile(mix_conv),
                  _layer_spec(wo, layer, (RWKV_WIDTH, D_MODEL), pipeline_mode=pl.Buffered(1)),
                  whole(wg), whole(wu), whole(wd),
                  _layer_spec(gains, layer, pipeline_mode=pl.Buffered(1)),
                  pl.BlockSpec((None, in_rows, w_in.shape[2]), lambda i: (next_layer, i, 0))],
        out_specs=[tile(x2), pl.BlockSpec((in_rows, w_in.shape[2]), lambda i: (i, 0))],
        out_shape=[jax.ShapeDtypeStruct(x2.shape, F32),
                   jax.ShapeDtypeStruct(w_in.shape[1:], BF16)],
        compiler_params=pltpu.CompilerParams(
            dimension_semantics=("parallel",), vmem_limit_bytes=VMEM_LIMIT),
        name="outffn",
    )(x2, y_rwkv, mix_conv, wo, wg, wu, wd, gains, w_in)


def _fold_low_rank(mu, w):
    return (1.0 - mu)[..., None] * w, mu[..., None] * w


def kernel(x, w_in, mu_rkv, mu_wag, w0, w1, w2, a0, a1, a2, g1, g2, k_k, k_a, r_k, lnx_g, lnx_b,
           sc_conv_w, cf_conv_w, cf_conv_b, cf_ln_g, cf_ln_b, w_o, w_gate, w_up, w_down,
           pre_mix_g, post_mix_g, pre_ffn_g, post_ffn_g):
    bsz, seq, d = x.shape
    depth = w_in.shape[0]
    w1a, w1b = _fold_low_rank(mu_wag[:, 0], w1)
    a1a, a1b = _fold_low_rank(mu_wag[:, 1], a1)
    g1a, g1b = _fold_low_rank(mu_wag[:, 2], g1)
    w_low = jnp.concatenate([w1a, a1a, g1a, w1b, a1b, g1b], axis=2).astype(BF16)
    w_inb = w_in[0].astype(BF16)
    pvec = jnp.concatenate([
        mu_rkv.reshape(depth, 3, RWKV_WIDTH),
        jnp.stack([w0, a0, k_k, k_a, r_k.reshape(depth, RWKV_WIDTH), lnx_g, lnx_b], axis=1),
        jnp.zeros((depth, _PVEC_ROWS - 10, RWKV_WIDTH), F32)], axis=1)
    lowrank_w = jnp.concatenate([w2, a2, g2], axis=1).astype(BF16)
    cf_p = jnp.stack([cf_conv_b, cf_ln_g, cf_ln_b], axis=1)
    gains = jnp.stack([pre_mix_g, post_mix_g, pre_ffn_g, post_ffn_g], axis=1)
    wo = w_o.astype(BF16)

    x2 = x.reshape(bsz * seq, d)
    for l in range(depth):
        proj, mix_conv, wg, wu, wd = _inproj_conv(x2, l, gains, w_inb, w_low, wo, sc_conv_w, cf_conv_w, cf_p,
                                                  w_gate, w_up, w_down, seq)
        proj3 = proj.reshape(bsz, seq, RWKV_COLS)
        y_rwkv = _rwkv(proj3, l, pvec, lowrank_w).reshape(bsz * seq, RWKV_WIDTH)
        x2, w_inb = _outffn(x2, y_rwkv, mix_conv, l, wo, wg, wu, wd, gains, w_in, min(l + 1, depth - 1))
    return x2.reshape(bsz, seq, d)
```

```python
import functools
import math

import jax
import jax.numpy as jnp
from jax import lax
from jax.experimental import pallas as pl
from jax.experimental.pallas import tpu as pltpu

D_MODEL = 1024
HEAD_DIM = 64
RWKV_WIDTH = 512
SC_WIDTH = 256
CF_WIDTH = 256
SC_KERNEL = 3
CF_KERNEL = 31
DECAY_RANK = 64
ICLR_RANK = 64
GATE_RANK = 128
LOW_WIDTH = DECAY_RANK + ICLR_RANK + GATE_RANK
IN_WIDTH = 3 * RWKV_WIDTH + 3 * SC_WIDTH + 2 * CF_WIDTH
PROJ_WIDTH = IN_WIDTH + 2 * LOW_WIDTH
NORM_EPS = 1e-6
LN_EPS = 1e-5
GN_EPS = 64e-5
MIX_WIDTH = RWKV_WIDTH + SC_WIDTH + CF_WIDTH
DECAY_SCALE = math.exp(-0.5)

CHUNK = 64
PAIR = 2 * HEAD_DIM
SLAB_CHUNKS = 2
ROW_TILE = 512
DOT_COLS = 256
CONV_COLS = 3 * SC_WIDTH + 2 * CF_WIDTH
RWKV_COLS = PROJ_WIDTH - CONV_COLS
CONV_ROWS = 64
CONV_BLOCKS_PER_DOT = 1
CONV_FIRST_DOT = 3
OUT_ROWS = 128
FFN_PARTS = 2
CF_HALO = 32
SC_HALO = 8
SUBLANES = 8
BF16_ROWS = 16
VMEM_LIMIT = 56 * 1024 * 1024

F32 = jnp.float32
BF16 = jnp.bfloat16


def _dot(a, b):
    return jnp.dot(a.astype(BF16), b.astype(BF16), preferred_element_type=F32)


def _dot_nt(a, b):
    return lax.dot_general(a.astype(BF16), b.astype(BF16), (((1,), (1,)), ((), ())),
                           preferred_element_type=F32)


def _dot_tn(a, b):
    return lax.dot_general(a.astype(BF16), b.astype(BF16), (((0,), (0,)), ((), ())),
                           preferred_element_type=F32)


def _rms(x, g):
    return x * lax.rsqrt(jnp.mean(x * x, axis=-1, keepdims=True) + NORM_EPS) * g


def _inproj_conv_kernel(x_ref, g_ref, w_ref, wl_ref, woc_ref, scw_ref, cfw_ref, cfp_ref,
                        wg32_ref, wu32_ref, wd32_ref, proj_ref, mixc_ref, wg_ref, wu_ref, wd_ref,
                        held, gb_buf, sc_buf, cf_buf, cf_sh, y_buf, *, tiles_per_seq):
    i = pl.program_id(0)
    tile = x_ref.shape[0]
    wg_ref[...] = wg32_ref[...].astype(BF16)
    wu_ref[...] = wu32_ref[...].astype(BF16)
    wd_ref[...] = wd32_ref[...].astype(BF16)

    @pl.when(i == 0)
    def _():
        held[...] = jnp.zeros(held.shape, F32)
        sc_buf[0:SC_HALO, :] = jnp.zeros((SC_HALO, SC_WIDTH), F32)
        cf_buf[0:CF_HALO, :] = jnp.zeros((CF_HALO, CF_WIDTH), F32)

    col = lambda n: slice(n * SC_WIDTH, (n + 1) * SC_WIDTH)
    gb_buf[...] = held[:, col(0)]
    sc_buf[SC_HALO:, :] = held[:, col(1)] * held[:, col(2)]
    cf_buf[CF_HALO:, :] = held[:, col(3)] * jax.nn.sigmoid(held[:, col(4)])

    n_sh = cf_sh.shape[1]
    for res in range(1, SUBLANES):
        cf_sh[res - 1] = cf_buf[pl.ds(res, n_sh), :]

    h = _rms(x_ref[...], g_ref[0:1, :]).astype(BF16)

    def project(j):
        c0 = j * DOT_COLS
        cols = slice(c0, c0 + DOT_COLS)
        if c0 < 3 * RWKV_WIDTH:
            proj_ref[:, cols] = jnp.dot(h, w_ref[:, cols], preferred_element_type=F32)
        elif c0 < RWKV_COLS:
            proj_ref[:, cols] = jnp.dot(h, wl_ref[:, c0 - 3 * RWKV_WIDTH:c0 - 3 * RWKV_WIDTH + DOT_COLS],
                                        preferred_element_type=F32)
        else:
            src = c0 - 2 * LOW_WIDTH
            held[:, c0 - RWKV_COLS:c0 - RWKV_COLS + DOT_COLS] = jnp.dot(
                h, w_ref[:, src:src + DOT_COLS], preferred_element_type=F32)

    def conv_rows(r0):
        acc = jnp.zeros((CONV_ROWS, SC_WIDTH), F32)
        for j in range(SC_KERNEL):
            acc = acc + scw_ref[j:j + 1, :] * sc_buf[pl.ds(r0 + SC_HALO - (SC_KERNEL - 1) + j, CONV_ROWS), :]
        y_buf[r0:r0 + CONV_ROWS, 0:SC_WIDTH] = (gb_buf[r0:r0 + CONV_ROWS, :] * acc).astype(y_buf.dtype)
        acc = jnp.zeros((CONV_ROWS, CF_WIDTH), F32) + cfp_ref[0:1, :]
        for j in range(CF_KERNEL):
            start = r0 + CF_HALO - (CF_KERNEL - 1) + j
            res = start % SUBLANES
            if res == 0:
                taps = cf_buf[pl.ds(start, CONV_ROWS), :]
            else:
                taps = cf_sh[res - 1, pl.ds(start - res, CONV_ROWS), :]
            acc = acc + cfw_ref[j:j + 1, :] * taps
        mu = jnp.mean(acc, axis=-1, keepdims=True)
        cen = acc - mu
        var = jnp.mean(cen * cen, axis=-1, keepdims=True)
        z = cen * lax.rsqrt(var + LN_EPS) * cfp_ref[1:2, :] + cfp_ref[2:3, :]
        y_buf[r0:r0 + CONV_ROWS, SC_WIDTH:] = (z * jax.nn.sigmoid(z)).astype(y_buf.dtype)

    n_conv = tile // CONV_ROWS
    done = 0
    for j in range(PROJ_WIDTH // DOT_COLS):
        project(j)
        upto = min(n_conv, done + CONV_BLOCKS_PER_DOT) if j >= CONV_FIRST_DOT else done
        for c in range(done, upto):
            conv_rows(c * CONV_ROWS)
            r1 = (c + 1) * CONV_ROWS
            if r1 % OUT_ROWS == 0:
                rows = slice(r1 - OUT_ROWS, r1)
                mixc_ref[rows, :] = jnp.dot(y_buf[rows, :], woc_ref[...], preferred_element_type=F32)
        done = upto
    assert done == n_conv

    keep = jnp.where(i % tiles_per_seq != 0, 1.0, 0.0).astype(F32)
    sc_buf[0:SC_HALO, :] = sc_buf[tile:tile + SC_HALO, :] * keep
    cf_buf[0:CF_HALO, :] = cf_buf[tile:tile + CF_HALO, :] * keep


def _layer_spec(a, layer, block=None, index=(0, 0), **kwargs):
    block = a.shape[1:] if block is None else block
    return pl.BlockSpec((None,) + tuple(block), lambda *_: (layer,) + tuple(index), **kwargs)


def _inproj_conv(x2, layer, gains, w, w_low, wo, sc_w, cf_w, cf_p, w_gate, w_up, w_down, seq):
    rows = x2.shape[0]
    n_tiles = rows // ROW_TILE
    ffn = w_gate.shape[2]
    full = lambda a: _layer_spec(a, layer)
    cur = lambda i: (jnp.minimum(i, n_tiles - 1), 0)
    up_rows = D_MODEL // n_tiles
    assert up_rows * n_tiles == D_MODEL and up_rows % BF16_ROWS == 0 and ffn % DOT_COLS == 0
    down_blocks = ffn // DOT_COLS
    assert down_blocks <= n_tiles
    up_blk = lambda i: jnp.minimum(i, n_tiles - 1)
    down_blk = lambda i: jnp.minimum(i, down_blocks - 1)
    return pl.pallas_call(
        functools.partial(_inproj_conv_kernel, tiles_per_seq=seq // ROW_TILE),
        grid=(n_tiles + 1,),
        in_specs=[
            pl.BlockSpec((ROW_TILE, D_MODEL), cur),
            full(gains),
            pl.BlockSpec(w.shape, lambda i: (0, 0), pipeline_mode=pl.Buffered(1)),
            _layer_spec(w_low, layer, pipeline_mode=pl.Buffered(1)),
            _layer_spec(wo, layer, (MIX_WIDTH - RWKV_WIDTH, D_MODEL), (1, 0)),
            full(sc_w), full(cf_w), full(cf_p),
            pl.BlockSpec((None, up_rows, ffn), lambda i: (layer, up_blk(i), 0)),
            pl.BlockSpec((None, up_rows, ffn), lambda i: (layer, up_blk(i), 0)),
            pl.BlockSpec((None, DOT_COLS, D_MODEL), lambda i: (layer, down_blk(i), 0)),
        ],
        out_specs=[pl.BlockSpec((ROW_TILE, RWKV_COLS), cur),
                   pl.BlockSpec((ROW_TILE, D_MODEL), lambda i: (jnp.maximum(i - 1, 0), 0)),
                   pl.BlockSpec((up_rows, ffn), lambda i: (up_blk(i), 0)),
                   pl.BlockSpec((up_rows, ffn), lambda i: (up_blk(i), 0)),
                   pl.BlockSpec((DOT_COLS, D_MODEL), lambda i: (down_blk(i), 0))],
        out_shape=[jax.ShapeDtypeStruct((rows, RWKV_COLS), F32),
                   jax.ShapeDtypeStruct((rows, D_MODEL), F32),
                   jax.ShapeDtypeStruct((D_MODEL, ffn), BF16),
                   jax.ShapeDtypeStruct((D_MODEL, ffn), BF16),
                   jax.ShapeDtypeStruct((ffn, D_MODEL), BF16)],
        scratch_shapes=[pltpu.VMEM((ROW_TILE, CONV_COLS), F32),
                        pltpu.VMEM((ROW_TILE, SC_WIDTH), F32),
                        pltpu.VMEM((SC_HALO + ROW_TILE, SC_WIDTH), F32),
                        pltpu.VMEM((CF_HALO + ROW_TILE, CF_WIDTH), F32),
                        pltpu.VMEM((SUBLANES - 1, CF_HALO + ROW_TILE - SUBLANES, CF_WIDTH), F32),
                        pltpu.VMEM((ROW_TILE, SC_WIDTH + CF_WIDTH), BF16)],
        compiler_params=pltpu.CompilerParams(
            dimension_semantics=("arbitrary",), vmem_limit_bytes=VMEM_LIMIT),
        name="inproj_conv",
    )(x2, gains, w, w_low, wo, sc_w, cf_w, cf_p, w_gate, w_up, w_down)


_MU_R, _MU_K, _MU_V, _W0, _A0, _K_K, _K_A, _R_K, _LNX_G, _LNX_B = range(10)
_PVEC_ROWS = 16


def _rwkv_kernel(r_ref, k_ref, v_ref, la_ref, lb_ref, pv_ref, lw_ref, o_ref,
                 ld_s, cum_s, iclr_s, gate_s, r_s, k_s, v_s, kk_s):
    seq = r_ref.shape[0]
    slab = SLAB_CHUNKS * CHUNK
    n_pairs = RWKV_WIDTH // PAIR
    n_slabs = seq // slab
    pv = pv_ref[...]
    prow = lambda i: pv[i:i + 1, :]
    w2 = lw_ref[0:DECAY_RANK, :]
    a2 = lw_ref[DECAY_RANK:DECAY_RANK + ICLR_RANK, :]
    g2 = lw_ref[DECAY_RANK + ICLR_RANK:, :]

    iota = lambda shape, dim: lax.broadcasted_iota(jnp.int32, shape, dim)
    srow, scol = iota((slab, slab), 0), iota((slab, slab), 1)
    tri_incl = ((srow // CHUNK == scol // CHUNK) & (srow >= scol)).astype(BF16)
    first_row = iota((slab, 1), 0) == 0
    head0 = iota((1, PAIR), 1) < HEAD_DIM
    head_c = (head0, jnp.logical_not(head0))
    row4 = iota((4 * CHUNK, PAIR), 0) % (2 * CHUNK)
    col4 = iota((4 * CHUNK, PAIR), 1) % CHUNK
    keep4 = jnp.where(row4 < CHUNK, row4, row4 - CHUNK + 1) > col4
    plane = iota((1, PAIR), 1) < CHUNK
    eye_p = iota((PAIR, PAIR), 0) == iota((PAIR, PAIR), 1)
    same_head = (iota((PAIR, PAIR), 0) // HEAD_DIM) == (iota((PAIR, PAIR), 1) // HEAD_DIM)
    zeros_b = jnp.zeros((CHUNK, PAIR), BF16)

    def head_sum(x):
        outs = []
        for p in range(x.shape[1] // PAIR):
            xp = x[:, p * PAIR:(p + 1) * PAIR]
            s0 = jnp.sum(jnp.where(head0, xp, 0.0), axis=-1, keepdims=True)
            s1 = jnp.sum(jnp.where(head0, 0.0, xp), axis=-1, keepdims=True)
            outs.append(jnp.where(head0, s0, s1))
        return outs[0] if len(outs) == 1 else jnp.concatenate(outs, axis=1)

    def with_shift(ref, i):
        t0 = pl.multiple_of(i * slab, slab)
        cur = ref[pl.ds(t0, slab), :]
        last = ref[pl.ds(jnp.maximum(t0 - 1, 0), 1), :] * jnp.where(i > 0, 1.0, 0.0).astype(F32)
        return cur, jnp.where(first_row, last, pltpu.roll(cur, 1, 0))

    def lowrank_dots(i):
        la, _ = with_shift(la_ref, i)
        _, lb_sh = with_shift(lb_ref, i)
        low = la + lb_sh
        w_pre = prow(_W0) + _dot(jnp.tanh(low[:, :DECAY_RANK]), w2)
        log_decay = -DECAY_SCALE * jax.nn.sigmoid(w_pre)
        iclr = jax.nn.sigmoid(prow(_A0) + _dot(low[:, DECAY_RANK:DECAY_RANK + ICLR_RANK], a2))
        gate = _dot(jax.nn.sigmoid(low[:, DECAY_RANK + ICLR_RANK:]), g2)
        return log_decay, iclr, gate

    def hand_over(log_decay, iclr, gate):
        ld_hi = log_decay.astype(BF16)
        ld_lo = (log_decay - ld_hi.astype(F32)).astype(BF16)
        ld_s[...] = log_decay
        cum_s[...] = (jnp.dot(tri_incl, ld_hi, preferred_element_type=F32)
                      + jnp.dot(tri_incl, ld_lo, preferred_element_type=F32))
        iclr_s[...] = iclr
        gate_s[...] = gate

    def mix_stage(i):
        def mixed(ref, mu):
            cur, sh = with_shift(ref, i)
            return cur + (sh - cur) * mu

        r_s[...] = mixed(r_ref, prow(_MU_R))
        k = mixed(k_ref, prow(_MU_K))
        k_s[...] = k
        v_s[...] = mixed(v_ref, prow(_MU_V))
        kk = k * prow(_K_K)
        kk_s[...] = kk * lax.rsqrt(jnp.maximum(head_sum(kk * kk), 1e-24))

    def slab_body(i, states):
        t0 = pl.multiple_of(i * slab, slab)
        r, k, v, kk = r_s[...], k_s[...], v_s[...], kk_s[...]
        log_decay, cum, iclr = ld_s[...], cum_s[...], iclr_s[...]
        k = k * (1.0 + (iclr - 1.0) * prow(_K_A))
        b_vec = kk * iclr

        e_neg = jnp.exp(-cum)
        a_t = -kk * jnp.exp(cum - log_decay)
        r_t = r * jnp.exp(cum)
        b_t = b_vec * e_neg
        k_t = k * e_neg
        bonus = head_sum(r * k * prow(_R_K)) * v

        units = [(j, p) for j in range(SLAB_CHUNKS) for p in range(n_pairs)]
        heads = range(2)
        blk = lambda x, u: x[u[0] * CHUNK:(u[0] + 1) * CHUNK, u[1] * PAIR:(u[1] + 1) * PAIR]
        cum_end = {u: cum[(u[0] + 1) * CHUNK - 1:(u[0] + 1) * CHUNK, u[1] * PAIR:(u[1] + 1) * PAIR]
                   for u in units}
        at = {u: [jnp.where(head_c[h], blk(a_t, u), 0.0) for h in heads] for u in units}
        rtb = {u: blk(r_t, u).astype(BF16) for u in units}
        vb = {u: blk(v, u).astype(BF16) for u in units}
        rt = {u: [jnp.where(head_c[h], rtb[u], zeros_b) for h in heads] for u in units}
        vh = {u: [jnp.where(head_c[h], vb[u], zeros_b) for h in heads] for u in units}
        xab = {u: [at[u][h].astype(BF16) for h in heads] for u in units}
        big = {u: jnp.where(keep4,
                            _dot_nt(jnp.concatenate([xab[u][0], rt[u][0], xab[u][1], rt[u][1]], axis=0),
                                    jnp.concatenate([blk(b_t, u), blk(k_t, u)], axis=0)), 0.0)
               for u in units}
        bot = {u: [big[u][(2 * h + 1) * CHUNK:(2 * h + 2) * CHUNK] for h in heads] for u in units}
        px = {u: [big[u][2 * h * CHUNK:(2 * h + 1) * CHUNK] for h in heads] for u in units}
        xa = at
        pxb = {u: [px[u][h].astype(BF16) for h in heads] for u in units}
        for _ in range(6):
            res = {u: [_dot(pxb[u][h][:, :CHUNK], jnp.concatenate([pxb[u][h], xab[u][h]], axis=1))
                       for h in heads] for u in units}
            px = {u: [res[u][h][:, :PAIR] + jnp.where(plane, 0.0, px[u][h]) for h in heads] for u in units}
            xa = {u: [res[u][h][:, PAIR:] + xa[u][h] for h in heads] for u in units}
            pxb = {u: [px[u][h].astype(BF16) for h in heads] for u in units}
            xab = {u: [xa[u][h].astype(BF16) for h in heads] for u in units}
        u_v = {u: [_dot(pxb[u][h], jnp.concatenate([zeros_b, vh[u][h]], axis=0)) for h in heads]
               for u in units}
        ry = {u: [_dot(bot[u][h],
                       jnp.concatenate([jnp.concatenate([xab[u][h], u_v[u][h].astype(BF16)], axis=1),
                                        jnp.concatenate([zeros_b, vh[u][h]], axis=1)], axis=0))
                  for h in heads] for u in units}
        mn = {}
        for u in units:
            e_end = jnp.exp(cum_end[u] - blk(cum, u))
            bk_h = jnp.concatenate([blk(b_vec, u) * e_end, blk(k, u) * e_end], axis=0)
            apuv = jnp.concatenate(
                [jnp.concatenate([xa[u][0] + xa[u][1], u_v[u][0] + u_v[u][1]], axis=1).astype(BF16),
                 jnp.concatenate([zeros_b, vb[u]], axis=1)], axis=0)
            mn[u] = _dot_tn(bk_h, apuv)
        states = list(states)
        gate = gate_s[...]
        next_low = None
        for u in units:
            j, p = u
            if j == 1 and next_low is None:
                next_low = lowrank_dots(jnp.minimum(i + 1, n_slabs - 1))
            m = mn[u][:, :PAIR] + jnp.where(eye_p, jnp.exp(cum_end[u]), 0.0)
            r_p = blk(r_t, u) + ry[u][0][:, :PAIR] + ry[u][1][:, :PAIR]
            out = _dot(jnp.concatenate([m, r_p], axis=0), states[p])
            states[p] = jnp.where(same_head, out[:PAIR] + mn[u][:, PAIR:], 0.0)
            y = out[PAIR:] + ry[u][0][:, PAIR:] + ry[u][1][:, PAIR:]
            yc = y - head_sum(y) * (1.0 / HEAD_DIM)
            var = head_sum(yc * yc) * (1.0 / HEAD_DIM)
            lanes = slice(p * PAIR, (p + 1) * PAIR)
            y = yc * lax.rsqrt(var + GN_EPS) * prow(_LNX_G)[:, lanes] + prow(_LNX_B)[:, lanes]
            o_ref[pl.ds(t0 + j * CHUNK, CHUNK), lanes] = ((y + blk(bonus, u)) * blk(gate, u)).astype(o_ref.dtype)
        hand_over(*next_low)
        mix_stage(jnp.minimum(i + 1, n_slabs - 1))
        return tuple(states)

    zero_state = jnp.zeros((PAIR, PAIR), F32)
    hand_over(*lowrank_dots(0))
    mix_stage(0)
    lax.fori_loop(0, n_slabs, slab_body, (zero_state,) * n_pairs)


def _rwkv(proj3, layer, pvec, lowrank_w):
    bsz, seq, _ = proj3.shape
    col = lambda blk: pl.BlockSpec((None, seq, RWKV_WIDTH), lambda b, blk=blk: (b, 0, blk))
    low = lambda blk: pl.BlockSpec((None, seq, LOW_WIDTH), lambda b, blk=blk: (b, 0, blk))
    full = lambda a: _layer_spec(a, layer)
    return pl.pallas_call(
        _rwkv_kernel,
        grid=(bsz,),
        in_specs=[col(0), col(1), col(2),
                  low(3 * RWKV_WIDTH // LOW_WIDTH), low(3 * RWKV_WIDTH // LOW_WIDTH + 1),
                  full(pvec), full(lowrank_w)],
        out_specs=pl.BlockSpec((None, seq, RWKV_WIDTH), lambda b: (b, 0, 0)),
        out_shape=jax.ShapeDtypeStruct((bsz, seq, RWKV_WIDTH), BF16),
        scratch_shapes=[pltpu.VMEM((SLAB_CHUNKS * CHUNK, RWKV_WIDTH), F32)] * 8,
        compiler_params=pltpu.CompilerParams(
            dimension_semantics=("parallel",), vmem_limit_bytes=VMEM_LIMIT),
        name="rwkv",
    )(proj3, proj3, proj3, proj3, proj3, pvec, lowrank_w)


def _outffn_kernel(x_ref, yr_ref, mixc_ref, wo_ref, wg_ref, wu_ref, wd_ref, g_ref, wn32_ref, o_ref, wn_ref):
    wn_ref[...] = wn32_ref[...].astype(BF16)
    half = x_ref.shape[0] // FFN_PARTS
    halves = [slice(n * half, (n + 1) * half) for n in range(FFN_PARTS)]
    mix = [jnp.dot(yr_ref[s, :], wo_ref[...], preferred_element_type=F32) + mixc_ref[s, :] for s in halves]
    x = [x_ref[s, :] + _rms(m, g_ref[1:2, :]) for s, m in zip(halves, mix)]
    h2 = [_rms(xs, g_ref[2:3, :]).astype(BF16) for xs in x]
    act = []
    for h in h2:
        gate = jnp.dot(h, wg_ref[...], preferred_element_type=F32)
        up = jnp.dot(h, wu_ref[...], preferred_element_type=F32)
        act.append((gate * jax.nn.sigmoid(gate) * up).astype(BF16))
    f = [jnp.dot(a, wd_ref[...], preferred_element_type=F32) for a in act]
    for s, xs, fs in zip(halves, x, f):
        o_ref[s, :] = xs + _rms(fs, g_ref[3:4, :])


def _outffn(x2, y_rwkv, mix_conv, layer, wo, wg, wu, wd, gains, w_in, next_layer):
    rows = x2.shape[0]
    n_tiles = rows // ROW_TILE
    in_rows = D_MODEL // n_tiles
    assert in_rows * n_tiles == D_MODEL and in_rows % BF16_ROWS == 0
    tile = lambda a: pl.BlockSpec((ROW_TILE, a.shape[1]), lambda i: (i, 0))
    whole = lambda a: pl.BlockSpec(a.shape, lambda i: (0, 0), pipeline_mode=pl.Buffered(1))
    return pl.pallas_call(
        _outffn_kernel,
        grid=(n_tiles,),
        in_specs=[tile(x2), tile(y_rwkv), tile(mix_conv),
                  _layer_spec(wo, layer, (RWKV_WIDTH, D_MODEL), pipeline_mode=pl.Buffered(1)),
                  whole(wg), whole(wu), whole(wd),
                  _layer_spec(gains, layer, pipeline_mode=pl.Buffered(1)),
                  pl.BlockSpec((None, in_rows, w_in.shape[2]), lambda i: (next_layer, i, 0))],
        out_specs=[tile(x2), pl.BlockSpec((in_rows, w_in.shape[2]), lambda i: (i, 0))],
        out_shape=[jax.ShapeDtypeStruct(x2.shape, F32),
                   jax.ShapeDtypeStruct(w_in.shape[1:], BF16)],
        compiler_params=pltpu.CompilerParams(
            dimension_semantics=("parallel",), vmem_limit_bytes=VMEM_LIMIT),
        name="outffn",
    )(x2, y_rwkv, mix_conv, wo, wg, wu, wd, gains, w_in)


def _fold_low_rank(mu, w):
    return (1.0 - mu)[..., None] * w, mu[..., None] * w


def kernel(x, w_in, mu_rkv, mu_wag, w0, w1, w2, a0, a1, a2, g1, g2, k_k, k_a, r_k, lnx_g, lnx_b,
           sc_conv_w, cf_conv_w, cf_conv_b, cf_ln_g, cf_ln_b, w_o, w_gate, w_up, w_down,
           pre_mix_g, post_mix_g, pre_ffn_g, post_ffn_g):
    bsz, seq, d = x.shape
    depth = w_in.shape[0]
    w1a, w1b = _fold_low_rank(mu_wag[:, 0], w1)
    a1a, a1b = _fold_low_rank(mu_wag[:, 1], a1)
    g1a, g1b = _fold_low_rank(mu_wag[:, 2], g1)
    w_low = jnp.concatenate([w1a, a1a, g1a, w1b, a1b, g1b], axis=2).astype(BF16)
    w_inb = w_in[0].astype(BF16)
    pvec = jnp.concatenate([
        mu_rkv.reshape(depth, 3, RWKV_WIDTH),
        jnp.stack([w0, a0, k_k, k_a, r_k.reshape(depth, RWKV_WIDTH), lnx_g, lnx_b], axis=1),
        jnp.zeros((depth, _PVEC_ROWS - 10, RWKV_WIDTH), F32)], axis=1)
    lowrank_w = jnp.concatenate([w2, a2, g2], axis=1).astype(BF16)
    cf_p = jnp.stack([cf_conv_b, cf_ln_g, cf_ln_b], axis=1)
    gains = jnp.stack([pre_mix_g, post_mix_g, pre_ffn_g, post_ffn_g], axis=1)
    wo = w_o.astype(BF16)

    x2 = x.reshape(bsz * seq, d)
    for l in range(depth):
        proj, mix_conv, wg, wu, wd = _inproj_conv(x2, l, gains, w_inb, w_low, wo, sc_conv_w, cf_conv_w, cf_p,
                                                  w_gate, w_up, w_down, seq)
        proj3 = proj.reshape(bsz, seq, RWKV_COLS)
        y_rwkv = _rwkv(proj3, l, pvec, lowrank_w).reshape(bsz * seq, RWKV_WIDTH)
        x2, w_inb = _outffn(x2, y_rwkv, mix_conv, l, wo, wg, wu, wd, gains, w_in, min(l + 1, depth - 1))
    return x2.reshape(bsz, seq, d)
```

```python
import functools
import math

import jax
import jax.numpy as jnp
from jax import lax
from jax.experimental import pallas as pl
from jax.experimental.pallas import tpu as pltpu

D_MODEL = 1024
HEAD_DIM = 64
RWKV_WIDTH = 512
SC_WIDTH = 256
CF_WIDTH = 256
SC_KERNEL = 3
CF_KERNEL = 31
DECAY_RANK = 64
ICLR_RANK = 64
GATE_RANK = 128
LOW_WIDTH = DECAY_RANK + ICLR_RANK + GATE_RANK
IN_WIDTH = 3 * RWKV_WIDTH + 3 * SC_WIDTH + 2 * CF_WIDTH
PROJ_WIDTH = IN_WIDTH + 2 * LOW_WIDTH
NORM_EPS = 1e-6
LN_EPS = 1e-5
GN_EPS = 64e-5
MIX_WIDTH = RWKV_WIDTH + SC_WIDTH + CF_WIDTH
DECAY_SCALE = math.exp(-0.5)

CHUNK = 64
PAIR = 2 * HEAD_DIM
SLAB_CHUNKS = 2
ROW_TILE = 512
DOT_COLS = 256
CONV_COLS = 3 * SC_WIDTH + 2 * CF_WIDTH
RWKV_COLS = PROJ_WIDTH - CONV_COLS
CONV_ROWS = 64
CONV_BLOCKS_PER_DOT = 1
CONV_FIRST_DOT = 3
OUT_ROWS = 128
FFN_PARTS = 2
CF_HALO = 32
SC_HALO = 8
SUBLANES = 8
BF16_ROWS = 16
VMEM_LIMIT = 56 * 1024 * 1024

F32 = jnp.float32
BF16 = jnp.bfloat16


def _dot(a, b):
    return jnp.dot(a.astype(BF16), b.astype(BF16), preferred_element_type=F32)


def _dot_nt(a, b):
    return lax.dot_general(a.astype(BF16), b.astype(BF16), (((1,), (1,)), ((), ())),
                           preferred_element_type=F32)


def _dot_tn(a, b):
    return lax.dot_general(a.astype(BF16), b.astype(BF16), (((0,), (0,)), ((), ())),
                           preferred_element_type=F32)


def _rms(x, g):
    return x * lax.rsqrt(jnp.mean(x * x, axis=-1, keepdims=True) + NORM_EPS) * g


def _inproj_conv_kernel(x_ref, g_ref, w_ref, wl_ref, woc_ref, scw_ref, cfw_ref, cfp_ref,
                        wg32_ref, wu32_ref, wd32_ref, proj_ref, mixc_ref, wg_ref, wu_ref, wd_ref,
                        held, gb_buf, sc_buf, cf_buf, cf_sh, y_buf, *, tiles_per_seq):
    i = pl.program_id(0)
    tile = x_ref.shape[0]
    wg_ref[...] = wg32_ref[...].astype(BF16)
    wu_ref[...] = wu32_ref[...].astype(BF16)
    wd_ref[...] = wd32_ref[...].astype(BF16)

    @pl.when(i == 0)
    def _():
        held[...] = jnp.zeros(held.shape, F32)
        sc_buf[0:SC_HALO, :] = jnp.zeros((SC_HALO, SC_WIDTH), F32)
        cf_buf[0:CF_HALO, :] = jnp.zeros((CF_HALO, CF_WIDTH), F32)

    col = lambda n: slice(n * SC_WIDTH, (n + 1) * SC_WIDTH)
    gb_buf[...] = held[:, col(0)]
    sc_buf[SC_HALO:, :] = held[:, col(1)] * held[:, col(2)]
    cf_buf[CF_HALO:, :] = held[:, col(3)] * jax.nn.sigmoid(held[:, col(4)])

    n_sh = cf_sh.shape[1]
    for res in range(1, SUBLANES):
        cf_sh[res - 1] = cf_buf[pl.ds(res, n_sh), :]

    h = _rms(x_ref[...], g_ref[0:1, :]).astype(BF16)

    def project(j):
        c0 = j * DOT_COLS
        cols = slice(c0, c0 + DOT_COLS)
        if c0 < 3 * RWKV_WIDTH:
            proj_ref[:, cols] = jnp.dot(h, w_ref[:, cols], preferred_element_type=F32).astype(proj_ref.dtype)
        elif c0 < RWKV_COLS:
            proj_ref[:, cols] = jnp.dot(h, wl_ref[:, c0 - 3 * RWKV_WIDTH:c0 - 3 * RWKV_WIDTH + DOT_COLS],
                                        preferred_element_type=F32).astype(proj_ref.dtype)
        else:
            src = c0 - 2 * LOW_WIDTH
            held[:, c0 - RWKV_COLS:c0 - RWKV_COLS + DOT_COLS] = jnp.dot(
                h, w_ref[:, src:src + DOT_COLS], preferred_element_type=F32)

    def conv_rows(r0):
        acc = jnp.zeros((CONV_ROWS, SC_WIDTH), F32)
        for j in range(SC_KERNEL):
            acc = acc + scw_ref[j:j + 1, :] * sc_buf[pl.ds(r0 + SC_HALO - (SC_KERNEL - 1) + j, CONV_ROWS), :]
        y_buf[r0:r0 + CONV_ROWS, 0:SC_WIDTH] = (gb_buf[r0:r0 + CONV_ROWS, :] * acc).astype(y_buf.dtype)
        acc = jnp.zeros((CONV_ROWS, CF_WIDTH), F32) + cfp_ref[0:1, :]
        for j in range(CF_KERNEL):
            start = r0 + CF_HALO - (CF_KERNEL - 1) + j
            res = start % SUBLANES
            if res == 0:
                taps = cf_buf[pl.ds(start, CONV_ROWS), :]
            else:
                taps = cf_sh[res - 1, pl.ds(start - res, CONV_ROWS), :]
            acc = acc + cfw_ref[j:j + 1, :] * taps
        mu = jnp.mean(acc, axis=-1, keepdims=True)
        cen = acc - mu
        var = jnp.mean(cen * cen, axis=-1, keepdims=True)
        z = cen * lax.rsqrt(var + LN_EPS) * cfp_ref[1:2, :] + cfp_ref[2:3, :]
        y_buf[r0:r0 + CONV_ROWS, SC_WIDTH:] = (z * jax.nn.sigmoid(z)).astype(y_buf.dtype)

    n_conv = tile // CONV_ROWS
    done = 0
    for j in range(PROJ_WIDTH // DOT_COLS):
        project(j)
        upto = min(n_conv, done + CONV_BLOCKS_PER_DOT) if j >= CONV_FIRST_DOT else done
        for c in range(done, upto):
            conv_rows(c * CONV_ROWS)
            r1 = (c + 1) * CONV_ROWS
            if r1 % OUT_ROWS == 0:
                rows = slice(r1 - OUT_ROWS, r1)
                mixc_ref[rows, :] = jnp.dot(y_buf[rows, :], woc_ref[...], preferred_element_type=F32)
        done = upto
    assert done == n_conv

    keep = jnp.where(i % tiles_per_seq != 0, 1.0, 0.0).astype(F32)
    sc_buf[0:SC_HALO, :] = sc_buf[tile:tile + SC_HALO, :] * keep
    cf_buf[0:CF_HALO, :] = cf_buf[tile:tile + CF_HALO, :] * keep


def _layer_spec(a, layer, block=None, index=(0, 0), **kwargs):
    block = a.shape[1:] if block is None else block
    return pl.BlockSpec((None,) + tuple(block), lambda *_: (layer,) + tuple(index), **kwargs)


def _inproj_conv(x2, layer, gains, w, w_low, wo, sc_w, cf_w, cf_p, w_gate, w_up, w_down, seq):
    rows = x2.shape[0]
    n_tiles = rows // ROW_TILE
    ffn = w_gate.shape[2]
    full = lambda a: _layer_spec(a, layer)
    cur = lambda i: (jnp.minimum(i, n_tiles - 1), 0)
    up_rows = D_MODEL // n_tiles
    assert up_rows * n_tiles == D_MODEL and up_rows % BF16_ROWS == 0 and ffn % DOT_COLS == 0
    down_blocks = ffn // DOT_COLS
    assert down_blocks <= n_tiles
    up_blk = lambda i: jnp.minimum(i, n_tiles - 1)
    down_blk = lambda i: jnp.minimum(i, down_blocks - 1)
    return pl.pallas_call(
        functools.partial(_inproj_conv_kernel, tiles_per_seq=seq // ROW_TILE),
        grid=(n_tiles + 1,),
        in_specs=[
            pl.BlockSpec((ROW_TILE, D_MODEL), cur),
            full(gains),
            pl.BlockSpec(w.shape, lambda i: (0, 0), pipeline_mode=pl.Buffered(1)),
            _layer_spec(w_low, layer, pipeline_mode=pl.Buffered(1)),
            _layer_spec(wo, layer, (MIX_WIDTH - RWKV_WIDTH, D_MODEL), (1, 0)),
            full(sc_w), full(cf_w), full(cf_p),
            pl.BlockSpec((None, up_rows, ffn), lambda i: (layer, up_blk(i), 0)),
            pl.BlockSpec((None, up_rows, ffn), lambda i: (layer, up_blk(i), 0)),
            pl.BlockSpec((None, DOT_COLS, D_MODEL), lambda i: (layer, down_blk(i), 0)),
        ],
        out_specs=[pl.BlockSpec((ROW_TILE, RWKV_COLS), cur),
                   pl.BlockSpec((ROW_TILE, D_MODEL), lambda i: (jnp.maximum(i - 1, 0), 0)),
                   pl.BlockSpec((up_rows, ffn), lambda i: (up_blk(i), 0)),
                   pl.BlockSpec((up_rows, ffn), lambda i: (up_blk(i), 0)),
                   pl.BlockSpec((DOT_COLS, D_MODEL), lambda i: (down_blk(i), 0))],
        out_shape=[jax.ShapeDtypeStruct((rows, RWKV_COLS), BF16),
                   jax.ShapeDtypeStruct((rows, D_MODEL), F32),
                   jax.ShapeDtypeStruct((D_MODEL, ffn), BF16),
                   jax.ShapeDtypeStruct((D_MODEL, ffn), BF16),
                   jax.ShapeDtypeStruct((ffn, D_MODEL), BF16)],
        scratch_shapes=[pltpu.VMEM((ROW_TILE, CONV_COLS), F32),
                        pltpu.VMEM((ROW_TILE, SC_WIDTH), F32),
                        pltpu.VMEM((SC_HALO + ROW_TILE, SC_WIDTH), F32),
                        pltpu.VMEM((CF_HALO + ROW_TILE, CF_WIDTH), F32),
                        pltpu.VMEM((SUBLANES - 1, CF_HALO + ROW_TILE - SUBLANES, CF_WIDTH), F32),
                        pltpu.VMEM((ROW_TILE, SC_WIDTH + CF_WIDTH), BF16)],
        compiler_params=pltpu.CompilerParams(
            dimension_semantics=("arbitrary",), vmem_limit_bytes=VMEM_LIMIT),
        name="inproj_conv",
    )(x2, gains, w, w_low, wo, sc_w, cf_w, cf_p, w_gate, w_up, w_down)


_MU_R, _MU_K, _MU_V, _W0, _A0, _K_K, _K_A, _R_K, _LNX_G, _LNX_B = range(10)
_PVEC_ROWS = 16


def _rwkv_kernel(r_ref, k_ref, v_ref, la_ref, lb_ref, pv_ref, lw_ref, o_ref, ld_s, cum_s, iclr_s, gate_s):
    seq = r_ref.shape[0]
    slab = SLAB_CHUNKS * CHUNK
    n_pairs = RWKV_WIDTH // PAIR
    n_slabs = seq // slab
    pv = pv_ref[...]
    prow = lambda i: pv[i:i + 1, :]
    w2 = lw_ref[0:DECAY_RANK, :]
    a2 = lw_ref[DECAY_RANK:DECAY_RANK + ICLR_RANK, :]
    g2 = lw_ref[DECAY_RANK + ICLR_RANK:, :]

    iota = lambda shape, dim: lax.broadcasted_iota(jnp.int32, shape, dim)
    srow, scol = iota((slab, slab), 0), iota((slab, slab), 1)
    tri_incl = ((srow // CHUNK == scol // CHUNK) & (srow >= scol)).astype(BF16)
    first_row = iota((slab, 1), 0) == 0
    head0 = iota((1, PAIR), 1) < HEAD_DIM
    head_c = (head0, jnp.logical_not(head0))
    row4 = iota((4 * CHUNK, PAIR), 0) % (2 * CHUNK)
    col4 = iota((4 * CHUNK, PAIR), 1) % CHUNK
    keep4 = jnp.where(row4 < CHUNK, row4, row4 - CHUNK + 1) > col4
    plane = iota((1, PAIR), 1) < CHUNK
    eye_p = iota((PAIR, PAIR), 0) == iota((PAIR, PAIR), 1)
    same_head = (iota((PAIR, PAIR), 0) // HEAD_DIM) == (iota((PAIR, PAIR), 1) // HEAD_DIM)
    zeros_b = jnp.zeros((CHUNK, PAIR), BF16)

    def head_sum(x):
        outs = []
        for p in range(x.shape[1] // PAIR):
            xp = x[:, p * PAIR:(p + 1) * PAIR]
            s0 = jnp.sum(jnp.where(head0, xp, 0.0), axis=-1, keepdims=True)
            s1 = jnp.sum(jnp.where(head0, 0.0, xp), axis=-1, keepdims=True)
            outs.append(jnp.where(head0, s0, s1))
        return outs[0] if len(outs) == 1 else jnp.concatenate(outs, axis=1)

    def with_shift(ref, i):
        t0 = pl.multiple_of(i * slab, slab)
        cur = ref[pl.ds(t0, slab), :].astype(F32)
        before = ref[pl.ds(pl.multiple_of(jnp.maximum(t0 - BF16_ROWS, 0), BF16_ROWS), BF16_ROWS), :].astype(F32)
        last = before[BF16_ROWS - 1:BF16_ROWS, :] * jnp.where(i > 0, 1.0, 0.0).astype(F32)
        return cur, jnp.where(first_row, last, pltpu.roll(cur, 1, 0))

    def lowrank_dots(i):
        la, _ = with_shift(la_ref, i)
        _, lb_sh = with_shift(lb_ref, i)
        low = la + lb_sh
        w_pre = prow(_W0) + _dot(jnp.tanh(low[:, :DECAY_RANK]), w2)
        log_decay = -DECAY_SCALE * jax.nn.sigmoid(w_pre)
        iclr = jax.nn.sigmoid(prow(_A0) + _dot(low[:, DECAY_RANK:DECAY_RANK + ICLR_RANK], a2))
        gate = _dot(jax.nn.sigmoid(low[:, DECAY_RANK + ICLR_RANK:]), g2)
        return log_decay, iclr, gate

    def hand_over(log_decay, iclr, gate):
        ld_hi = log_decay.astype(BF16)
        ld_lo = (log_decay - ld_hi.astype(F32)).astype(BF16)
        ld_s[...] = log_decay
        cum_s[...] = (jnp.dot(tri_incl, ld_hi, preferred_element_type=F32)
                      + jnp.dot(tri_incl, ld_lo, preferred_element_type=F32))
        iclr_s[...] = iclr
        gate_s[...] = gate

    def slab_body(i, states):
        t0 = pl.multiple_of(i * slab, slab)

        def mixed(ref, mu):
            cur, sh = with_shift(ref, i)
            return cur + (sh - cur) * mu

        r = mixed(r_ref, prow(_MU_R))
        k = mixed(k_ref, prow(_MU_K))
        v = mixed(v_ref, prow(_MU_V))
        log_decay, cum, iclr = ld_s[...], cum_s[...], iclr_s[...]

        kk = k * prow(_K_K)
        kk = kk * lax.rsqrt(jnp.maximum(head_sum(kk * kk), 1e-24))
        k = k * (1.0 + (iclr - 1.0) * prow(_K_A))
        b_vec = kk * iclr

        e_neg = jnp.exp(-cum)
        a_t = -kk * jnp.exp(cum - log_decay)
        r_t = r * jnp.exp(cum)
        b_t = b_vec * e_neg
        k_t = k * e_neg
        bonus = head_sum(r * k * prow(_R_K)) * v

        units = [(j, p) for j in range(SLAB_CHUNKS) for p in range(n_pairs)]
        heads = range(2)
        blk = lambda x, u: x[u[0] * CHUNK:(u[0] + 1) * CHUNK, u[1] * PAIR:(u[1] + 1) * PAIR]
        cum_end = {u: cum[(u[0] + 1) * CHUNK - 1:(u[0] + 1) * CHUNK, u[1] * PAIR:(u[1] + 1) * PAIR]
                   for u in units}
        at = {u: [jnp.where(head_c[h], blk(a_t, u), 0.0) for h in heads] for u in units}
        rtb = {u: blk(r_t, u).astype(BF16) for u in units}
        vb = {u: blk(v, u).astype(BF16) for u in units}
        rt = {u: [jnp.where(head_c[h], rtb[u], zeros_b) for h in heads] for u in units}
        vh = {u: [jnp.where(head_c[h], vb[u], zeros_b) for h in heads] for u in units}
        xab = {u: [at[u][h].astype(BF16) for h in heads] for u in units}
        big = {u: jnp.where(keep4,
                            _dot_nt(jnp.concatenate([xab[u][0], rt[u][0], xab[u][1], rt[u][1]], axis=0),
                                    jnp.concatenate([blk(b_t, u), blk(k_t, u)], axis=0)), 0.0)
               for u in units}
        bot = {u: [big[u][(2 * h + 1) * CHUNK:(2 * h + 2) * CHUNK] for h in heads] for u in units}
        px = {u: [big[u][2 * h * CHUNK:(2 * h + 1) * CHUNK] for h in heads] for u in units}
        xa = at
        pxb = {u: [px[u][h].astype(BF16) for h in heads] for u in units}
        for _ in range(6):
            res = {u: [_dot(pxb[u][h][:, :CHUNK], jnp.concatenate([pxb[u][h], xab[u][h]], axis=1))
                       for h in heads] for u in units}
            px = {u: [res[u][h][:, :PAIR] + jnp.where(plane, 0.0, px[u][h]) for h in heads] for u in units}
            xa = {u: [res[u][h][:, PAIR:] + xa[u][h] for h in heads] for u in units}
            pxb = {u: [px[u][h].astype(BF16) for h in heads] for u in units}
            xab = {u: [xa[u][h].astype(BF16) for h in heads] for u in units}
        u_v = {u: [_dot(pxb[u][h], jnp.concatenate([zeros_b, vh[u][h]], axis=0)) for h in heads]
               for u in units}
        ry = {u: [_dot(bot[u][h],
                       jnp.concatenate([jnp.concatenate([xab[u][h], u_v[u][h].astype(BF16)], axis=1),
                                        jnp.concatenate([zeros_b, vh[u][h]], axis=1)], axis=0))
                  for h in heads] for u in units}
        mn = {}
        for u in units:
            e_end = jnp.exp(cum_end[u] - blk(cum, u))
            bk_h = jnp.concatenate([blk(b_vec, u) * e_end, blk(k, u) * e_end], axis=0)
            apuv = jnp.concatenate(
                [jnp.concatenate([xa[u][0] + xa[u][1], u_v[u][0] + u_v[u][1]], axis=1).astype(BF16),
                 jnp.concatenate([zeros_b, vb[u]], axis=1)], axis=0)
            mn[u] = _dot_tn(bk_h, apuv)
        states = list(states)
        gate = gate_s[...]
        next_low = None
        for u in units:
            j, p = u
            if j == 1 and next_low is None:
                next_low = lowrank_dots(jnp.minimum(i + 1, n_slabs - 1))
            m = mn[u][:, :PAIR] + jnp.where(eye_p, jnp.exp(cum_end[u]), 0.0)
            r_p = blk(r_t, u) + ry[u][0][:, :PAIR] + ry[u][1][:, :PAIR]
            out = _dot(jnp.concatenate([m, r_p], axis=0), states[p])
            states[p] = jnp.where(same_head, out[:PAIR] + mn[u][:, PAIR:], 0.0)
            y = out[PAIR:] + ry[u][0][:, PAIR:] + ry[u][1][:, PAIR:]
            yc = y - head_sum(y) * (1.0 / HEAD_DIM)
            var = head_sum(yc * yc) * (1.0 / HEAD_DIM)
            lanes = slice(p * PAIR, (p + 1) * PAIR)
            y = yc * lax.rsqrt(var + GN_EPS) * prow(_LNX_G)[:, lanes] + prow(_LNX_B)[:, lanes]
            o_ref[pl.ds(t0 + j * CHUNK, CHUNK), lanes] = ((y + blk(bonus, u)) * blk(gate, u)).astype(o_ref.dtype)
        hand_over(*next_low)
        return tuple(states)

    zero_state = jnp.zeros((PAIR, PAIR), F32)
    hand_over(*lowrank_dots(0))
    lax.fori_loop(0, n_slabs, slab_body, (zero_state,) * n_pairs)


def _rwkv(proj3, layer, pvec, lowrank_w):
    bsz, seq, _ = proj3.shape
    col = lambda blk: pl.BlockSpec((None, seq, RWKV_WIDTH), lambda b, blk=blk: (b, 0, blk))
    low = lambda blk: pl.BlockSpec((None, seq, LOW_WIDTH), lambda b, blk=blk: (b, 0, blk))
    full = lambda a: _layer_spec(a, layer)
    return pl.pallas_call(
        _rwkv_kernel,
        grid=(bsz,),
        in_specs=[col(0), col(1), col(2),
                  low(3 * RWKV_WIDTH // LOW_WIDTH), low(3 * RWKV_WIDTH // LOW_WIDTH + 1),
                  full(pvec), full(lowrank_w)],
        out_specs=pl.BlockSpec((None, seq, RWKV_WIDTH), lambda b: (b, 0, 0)),
        out_shape=jax.ShapeDtypeStruct((bsz, seq, RWKV_WIDTH), BF16),
        scratch_shapes=[pltpu.VMEM((SLAB_CHUNKS * CHUNK, RWKV_WIDTH), F32)] * 4,
        compiler_params=pltpu.CompilerParams(
            dimension_semantics=("parallel",), vmem_limit_bytes=VMEM_LIMIT),
        name="rwkv",
    )(proj3, proj3, proj3, proj3, proj3, pvec, lowrank_w)


def _outffn_kernel(x_ref, yr_ref, mixc_ref, wo_ref, wg_ref, wu_ref, wd_ref, g_ref, wn32_ref, o_ref, wn_ref):
    wn_ref[...] = wn32_ref[...].astype(BF16)
    half = x_ref.shape[0] // FFN_PARTS
    halves = [slice(n * half, (n + 1) * half) for n in range(FFN_PARTS)]
    mix = [jnp.dot(yr_ref[s, :], wo_ref[...], preferred_element_type=F32) + mixc_ref[s, :] for s in halves]
    x = [x_ref[s, :] + _rms(m, g_ref[1:2, :]) for s, m in zip(halves, mix)]
    h2 = [_rms(xs, g_ref[2:3, :]).astype(BF16) for xs in x]
    act = []
    for h in h2:
        gate = jnp.dot(h, wg_ref[...], preferred_element_type=F32)
        up = jnp.dot(h, wu_ref[...], preferred_element_type=F32)
        act.append((gate * jax.nn.sigmoid(gate) * up).astype(BF16))
    f = [jnp.dot(a, wd_ref[...], preferred_element_type=F32) for a in act]
    for s, xs, fs in zip(halves, x, f):
        o_ref[s, :] = xs + _rms(fs, g_ref[3:4, :])


def _outffn(x2, y_rwkv, mix_conv, layer, wo, wg, wu, wd, gains, w_in, next_layer):
    rows = x2.shape[0]
    n_tiles = rows // ROW_TILE
    in_rows = D_MODEL // n_tiles
    assert in_rows * n_tiles == D_MODEL and in_rows % BF16_ROWS == 0
    tile = lambda a: pl.BlockSpec((ROW_TILE, a.shape[1]), lambda i: (i, 0))
    whole = lambda a: pl.BlockSpec(a.shape, lambda i: (0, 0), pipeline_mode=pl.Buffered(1))
    return pl.pallas_call(
        _outffn_kernel,
        grid=(n_tiles,),
        in_specs=[tile(x2), tile(y_rwkv), tile(mix_conv),
                  _layer_spec(wo, layer, (RWKV_WIDTH, D_MODEL), pipeline_mode=pl.Buffered(1)),
                  whole(wg), whole(wu), whole(wd),
                  _layer_spec(gains, layer, pipeline_mode=pl.Buffered(1)),
                  pl.BlockSpec((None, in_rows, w_in.shape[2]), lambda i: (next_layer, i, 0))],
        out_specs=[tile(x2), pl.BlockSpec((in_rows, w_in.shape[2]), lambda i: (i, 0))],
        out_shape=[jax.ShapeDtypeStruct(x2.shape, F32),
                   jax.ShapeDtypeStruct(w_in.shape[1:], BF16)],
        compiler_params=pltpu.CompilerParams(
            dimension_semantics=("parallel",), vmem_limit_bytes=VMEM_LIMIT),
        name="outffn",
    )(x2, y_rwkv, mix_conv, wo, wg, wu, wd, gains, w_in)


def _fold_low_rank(mu, w):
    return (1.0 - mu)[..., None] * w, mu[..., None] * w


def kernel(x, w_in, mu_rkv, mu_wag, w0, w1, w2, a0, a1, a2, g1, g2, k_k, k_a, r_k, lnx_g, lnx_b,
           sc_conv_w, cf_conv_w, cf_conv_b, cf_ln_g, cf_ln_b, w_o, w_gate, w_up, w_down,
           pre_mix_g, post_mix_g, pre_ffn_g, post_ffn_g):
    bsz, seq, d = x.shape
    depth = w_in.shape[0]
    w1a, w1b = _fold_low_rank(mu_wag[:, 0], w1)
    a1a, a1b = _fold_low_rank(mu_wag[:, 1], a1)
    g1a, g1b = _fold_low_rank(mu_wag[:, 2], g1)
    w_low = jnp.concatenate([w1a, a1a, g1a, w1b, a1b, g1b], axis=2).astype(BF16)
    w_inb = w_in[0].astype(BF16)
    pvec = jnp.concatenate([
        mu_rkv.reshape(depth, 3, RWKV_WIDTH),
        jnp.stack([w0, a0, k_k, k_a, r_k.reshape(depth, RWKV_WIDTH), lnx_g, lnx_b], axis=1),
        jnp.zeros((depth, _PVEC_ROWS - 10, RWKV_WIDTH), F32)], axis=1)
    lowrank_w = jnp.concatenate([w2, a2, g2], axis=1).astype(BF16)
    cf_p = jnp.stack([cf_conv_b, cf_ln_g, cf_ln_b], axis=1)
    gains = jnp.stack([pre_mix_g, post_mix_g, pre_ffn_g, post_ffn_g], axis=1)
    wo = w_o.astype(BF16)

    x2 = x.reshape(bsz * seq, d)
    for l in range(depth):
        proj, mix_conv, wg, wu, wd = _inproj_conv(x2, l, gains, w_inb, w_low, wo, sc_conv_w, cf_conv_w, cf_p,
                                                  w_gate, w_up, w_down, seq)
        proj3 = proj.reshape(bsz, seq, RWKV_COLS)
        y_rwkv = _rwkv(proj3, l, pvec, lowrank_w).reshape(bsz * seq, RWKV_WIDTH)
        x2, w_inb = _outffn(x2, y_rwkv, mix_conv, l, wo, wg, wu, wd, gains, w_in, min(l + 1, depth - 1))
    return x2.reshape(bsz, seq, d)
```

```python
import functools
import math

import jax
import jax.numpy as jnp
from jax import lax
from jax.experimental import pallas as pl
from jax.experimental.pallas import tpu as pltpu

D_MODEL = 1024
HEAD_DIM = 64
RWKV_WIDTH = 512
SC_WIDTH = 256
CF_WIDTH = 256
SC_KERNEL = 3
CF_KERNEL = 31
DECAY_RANK = 64
ICLR_RANK = 64
GATE_RANK = 128
LOW_WIDTH = DECAY_RANK + ICLR_RANK + GATE_RANK
IN_WIDTH = 3 * RWKV_WIDTH + 3 * SC_WIDTH + 2 * CF_WIDTH
PROJ_WIDTH = IN_WIDTH + 2 * LOW_WIDTH
NORM_EPS = 1e-6
LN_EPS = 1e-5
GN_EPS = 64e-5
MIX_WIDTH = RWKV_WIDTH + SC_WIDTH + CF_WIDTH
DECAY_SCALE = math.exp(-0.5)

CHUNK = 64
PAIR = 2 * HEAD_DIM
SLAB_CHUNKS = 2
ROW_TILE = 512
DOT_COLS = 256
CONV_COLS = 3 * SC_WIDTH + 2 * CF_WIDTH
RWKV_COLS = PROJ_WIDTH - CONV_COLS
CONV_ROWS = 64
CONV_BLOCKS_PER_DOT = 1
CONV_FIRST_DOT = 3
OUT_ROWS = 128
FFN_PARTS = 2
CF_HALO = 32
SC_HALO = 8
SUBLANES = 8
BF16_ROWS = 16
VMEM_LIMIT = 56 * 1024 * 1024

F32 = jnp.float32
BF16 = jnp.bfloat16


def _dot(a, b):
    return jnp.dot(a.astype(BF16), b.astype(BF16), preferred_element_type=F32)


def _dot_nt(a, b):
    return lax.dot_general(a.astype(BF16), b.astype(BF16), (((1,), (1,)), ((), ())),
                           preferred_element_type=F32)


def _dot_tn(a, b):
    return lax.dot_general(a.astype(BF16), b.astype(BF16), (((0,), (0,)), ((), ())),
                           preferred_element_type=F32)


def _rms(x, g):
    return x * lax.rsqrt(jnp.mean(x * x, axis=-1, keepdims=True) + NORM_EPS) * g


def _inproj_conv_kernel(x_ref, g_ref, w_ref, wl_ref, woc_ref, scw_ref, cfw_ref, cfp_ref,
                        wg32_ref, wu32_ref, wd32_ref, proj_ref, mixc_ref, wg_ref, wu_ref, wd_ref,
                        held, gb_buf, sc_buf, cf_buf, cf_sh, y_buf, *, tiles_per_seq):
    i = pl.program_id(0)
    tile = x_ref.shape[0]
    wg_ref[...] = wg32_ref[...].astype(BF16)
    wu_ref[...] = wu32_ref[...].astype(BF16)
    wd_ref[...] = wd32_ref[...].astype(BF16)

    @pl.when(i == 0)
    def _():
        held[...] = jnp.zeros(held.shape, F32)
        sc_buf[0:SC_HALO, :] = jnp.zeros((SC_HALO, SC_WIDTH), F32)
        cf_buf[0:CF_HALO, :] = jnp.zeros((CF_HALO, CF_WIDTH), F32)

    col = lambda n: slice(n * SC_WIDTH, (n + 1) * SC_WIDTH)
    gb_buf[...] = held[:, col(0)]
    sc_buf[SC_HALO:, :] = held[:, col(1)] * held[:, col(2)]
    cf_buf[CF_HALO:, :] = held[:, col(3)] * jax.nn.sigmoid(held[:, col(4)])

    n_sh = cf_sh.shape[1]
    for res in range(1, SUBLANES):
        cf_sh[res - 1] = cf_buf[pl.ds(res, n_sh), :]

    h = _rms(x_ref[...], g_ref[0:1, :]).astype(BF16)

    def project(j):
        c0 = j * DOT_COLS
        cols = slice(c0, c0 + DOT_COLS)
        if c0 < 3 * RWKV_WIDTH:
            proj_ref[:, cols] = jnp.dot(h, w_ref[:, cols], preferred_element_type=F32)
        elif c0 < RWKV_COLS:
            proj_ref[:, cols] = jnp.dot(h, wl_ref[:, c0 - 3 * RWKV_WIDTH:c0 - 3 * RWKV_WIDTH + DOT_COLS],
                                        preferred_element_type=F32)
        else:
            src = c0 - 2 * LOW_WIDTH
            held[:, c0 - RWKV_COLS:c0 - RWKV_COLS + DOT_COLS] = jnp.dot(
                h, w_ref[:, src:src + DOT_COLS], preferred_element_type=F32)

    def conv_rows(r0):
        acc = jnp.zeros((CONV_ROWS, SC_WIDTH), F32)
        for j in range(SC_KERNEL):
            acc = acc + scw_ref[j:j + 1, :] * sc_buf[pl.ds(r0 + SC_HALO - (SC_KERNEL - 1) + j, CONV_ROWS), :]
        y_buf[r0:r0 + CONV_ROWS, 0:SC_WIDTH] = (gb_buf[r0:r0 + CONV_ROWS, :] * acc).astype(y_buf.dtype)
        acc = jnp.zeros((CONV_ROWS, CF_WIDTH), F32) + cfp_ref[0:1, :]
        for j in range(CF_KERNEL):
            start = r0 + CF_HALO - (CF_KERNEL - 1) + j
            res = start % SUBLANES
            if res == 0:
                taps = cf_buf[pl.ds(start, CONV_ROWS), :]
            else:
                taps = cf_sh[res - 1, pl.ds(start - res, CONV_ROWS), :]
            acc = acc + cfw_ref[j:j + 1, :] * taps
        mu = jnp.mean(acc, axis=-1, keepdims=True)
        cen = acc - mu
        var = jnp.mean(cen * cen, axis=-1, keepdims=True)
        z = cen * lax.rsqrt(var + LN_EPS) * cfp_ref[1:2, :] + cfp_ref[2:3, :]
        y_buf[r0:r0 + CONV_ROWS, SC_WIDTH:] = (z * jax.nn.sigmoid(z)).astype(y_buf.dtype)

    n_conv = tile // CONV_ROWS
    done = 0
    for j in range(PROJ_WIDTH // DOT_COLS):
        project(j)
        upto = min(n_conv, done + CONV_BLOCKS_PER_DOT) if j >= CONV_FIRST_DOT else done
        for c in range(done, upto):
            conv_rows(c * CONV_ROWS)
            r1 = (c + 1) * CONV_ROWS
            if r1 % OUT_ROWS == 0:
                rows = slice(r1 - OUT_ROWS, r1)
                mixc_ref[rows, :] = jnp.dot(y_buf[rows, :], woc_ref[...], preferred_element_type=F32)
        done = upto
    assert done == n_conv

    keep = jnp.where(i % tiles_per_seq != 0, 1.0, 0.0).astype(F32)
    sc_buf[0:SC_HALO, :] = sc_buf[tile:tile + SC_HALO, :] * keep
    cf_buf[0:CF_HALO, :] = cf_buf[tile:tile + CF_HALO, :] * keep


def _layer_spec(a, layer, block=None, index=(0, 0), **kwargs):
    block = a.shape[1:] if block is None else block
    return pl.BlockSpec((None,) + tuple(block), lambda *_: (layer,) + tuple(index), **kwargs)


def _inproj_conv(x2, layer, gains, w, w_low, wo, sc_w, cf_w, cf_p, w_gate, w_up, w_down, seq):
    rows = x2.shape[0]
    n_tiles = rows // ROW_TILE
    ffn = w_gate.shape[2]
    full = lambda a: _layer_spec(a, layer)
    cur = lambda i: (jnp.minimum(i, n_tiles - 1), 0)
    up_rows = D_MODEL // n_tiles
    assert up_rows * n_tiles == D_MODEL and up_rows % BF16_ROWS == 0 and ffn % DOT_COLS == 0
    down_blocks = ffn // DOT_COLS
    assert down_blocks <= n_tiles
    up_blk = lambda i: jnp.minimum(i, n_tiles - 1)
    down_blk = lambda i: jnp.minimum(i, down_blocks - 1)
    return pl.pallas_call(
        functools.partial(_inproj_conv_kernel, tiles_per_seq=seq // ROW_TILE),
        grid=(n_tiles + 1,),
        in_specs=[
            pl.BlockSpec((ROW_TILE, D_MODEL), cur),
            full(gains),
            pl.BlockSpec(w.shape, lambda i: (0, 0), pipeline_mode=pl.Buffered(1)),
            _layer_spec(w_low, layer, pipeline_mode=pl.Buffered(1)),
            _layer_spec(wo, layer, (MIX_WIDTH - RWKV_WIDTH, D_MODEL), (1, 0)),
            full(sc_w), full(cf_w), full(cf_p),
            pl.BlockSpec((None, up_rows, ffn), lambda i: (layer, up_blk(i), 0)),
            pl.BlockSpec((None, up_rows, ffn), lambda i: (layer, up_blk(i), 0)),
            pl.BlockSpec((None, DOT_COLS, D_MODEL), lambda i: (layer, down_blk(i), 0)),
        ],
        out_specs=[pl.BlockSpec((ROW_TILE, RWKV_COLS), cur),
                   pl.BlockSpec((ROW_TILE, D_MODEL), lambda i: (jnp.maximum(i - 1, 0), 0)),
                   pl.BlockSpec((up_rows, ffn), lambda i: (up_blk(i), 0)),
                   pl.BlockSpec((up_rows, ffn), lambda i: (up_blk(i), 0)),
                   pl.BlockSpec((DOT_COLS, D_MODEL), lambda i: (down_blk(i), 0))],
        out_shape=[jax.ShapeDtypeStruct((rows, RWKV_COLS), F32),
                   jax.ShapeDtypeStruct((rows, D_MODEL), F32),
                   jax.ShapeDtypeStruct((D_MODEL, ffn), BF16),
                   jax.ShapeDtypeStruct((D_MODEL, ffn), BF16),
                   jax.ShapeDtypeStruct((ffn, D_MODEL), BF16)],
        scratch_shapes=[pltpu.VMEM((ROW_TILE, CONV_COLS), F32),
                        pltpu.VMEM((ROW_TILE, SC_WIDTH), F32),
                        pltpu.VMEM((SC_HALO + ROW_TILE, SC_WIDTH), F32),
                        pltpu.VMEM((CF_HALO + ROW_TILE, CF_WIDTH), F32),
                        pltpu.VMEM((SUBLANES - 1, CF_HALO + ROW_TILE - SUBLANES, CF_WIDTH), F32),
                        pltpu.VMEM((ROW_TILE, SC_WIDTH + CF_WIDTH), BF16)],
        compiler_params=pltpu.CompilerParams(
            dimension_semantics=("arbitrary",), vmem_limit_bytes=VMEM_LIMIT,
            allow_input_fusion=[False, False, True, True, True] + [False] * 6),
        name="inproj_conv",
    )(x2, gains, w, w_low, wo, sc_w, cf_w, cf_p, w_gate, w_up, w_down)


_MU_R, _MU_K, _MU_V, _W0, _A0, _K_K, _K_A, _R_K, _LNX_G, _LNX_B = range(10)
_PVEC_ROWS = 16


def _rwkv_kernel(r_ref, k_ref, v_ref, la_ref, lb_ref, pv_ref, lw_ref, o_ref, ld_s, cum_s, iclr_s, gate_s):
    seq = r_ref.shape[0]
    slab = SLAB_CHUNKS * CHUNK
    n_pairs = RWKV_WIDTH // PAIR
    n_slabs = seq // slab
    pv = pv_ref[...]
    prow = lambda i: pv[i:i + 1, :]
    w2 = lw_ref[0:DECAY_RANK, :]
    a2 = lw_ref[DECAY_RANK:DECAY_RANK + ICLR_RANK, :]
    g2 = lw_ref[DECAY_RANK + ICLR_RANK:, :]

    iota = lambda shape, dim: lax.broadcasted_iota(jnp.int32, shape, dim)
    srow, scol = iota((slab, slab), 0), iota((slab, slab), 1)
    tri_incl = ((srow // CHUNK == scol // CHUNK) & (srow >= scol)).astype(BF16)
    first_row = iota((slab, 1), 0) == 0
    head0 = iota((1, PAIR), 1) < HEAD_DIM
    head_c = (head0, jnp.logical_not(head0))
    row4 = iota((4 * CHUNK, PAIR), 0) % (2 * CHUNK)
    col4 = iota((4 * CHUNK, PAIR), 1) % CHUNK
    keep4 = jnp.where(row4 < CHUNK, row4, row4 - CHUNK + 1) > col4
    plane = iota((1, PAIR), 1) < CHUNK
    eye_p = iota((PAIR, PAIR), 0) == iota((PAIR, PAIR), 1)
    same_head = (iota((PAIR, PAIR), 0) // HEAD_DIM) == (iota((PAIR, PAIR), 1) // HEAD_DIM)
    zeros_b = jnp.zeros((CHUNK, PAIR), BF16)

    def head_sum(x):
        outs = []
        for p in range(x.shape[1] // PAIR):
            xp = x[:, p * PAIR:(p + 1) * PAIR]
            s0 = jnp.sum(jnp.where(head0, xp, 0.0), axis=-1, keepdims=True)
            s1 = jnp.sum(jnp.where(head0, 0.0, xp), axis=-1, keepdims=True)
            outs.append(jnp.where(head0, s0, s1))
        return outs[0] if len(outs) == 1 else jnp.concatenate(outs, axis=1)

    def with_shift(ref, i):
        t0 = pl.multiple_of(i * slab, slab)
        cur = ref[pl.ds(t0, slab), :]
        last = ref[pl.ds(jnp.maximum(t0 - 1, 0), 1), :] * jnp.where(i > 0, 1.0, 0.0).astype(F32)
        return cur, jnp.where(first_row, last, pltpu.roll(cur, 1, 0))

    def lowrank_dots(i):
        la, _ = with_shift(la_ref, i)
        _, lb_sh = with_shift(lb_ref, i)
        low = la + lb_sh
        w_pre = prow(_W0) + _dot(jnp.tanh(low[:, :DECAY_RANK]), w2)
        log_decay = -DECAY_SCALE * jax.nn.sigmoid(w_pre)
        iclr = jax.nn.sigmoid(prow(_A0) + _dot(low[:, DECAY_RANK:DECAY_RANK + ICLR_RANK], a2))
        gate = _dot(jax.nn.sigmoid(low[:, DECAY_RANK + ICLR_RANK:]), g2)
        return log_decay, iclr, gate

    def hand_over(log_decay, iclr, gate):
        ld_hi = log_decay.astype(BF16)
        ld_lo = (log_decay - ld_hi.astype(F32)).astype(BF16)
        ld_s[...] = log_decay
        cum_s[...] = (jnp.dot(tri_incl, ld_hi, preferred_element_type=F32)
                      + jnp.dot(tri_incl, ld_lo, preferred_element_type=F32))
        iclr_s[...] = iclr
        gate_s[...] = gate

    def slab_body(i, states):
        t0 = pl.multiple_of(i * slab, slab)

        def mixed(ref, mu):
            cur, sh = with_shift(ref, i)
            return cur + (sh - cur) * mu

        r = mixed(r_ref, prow(_MU_R))
        k = mixed(k_ref, prow(_MU_K))
        v = mixed(v_ref, prow(_MU_V))
        log_decay, cum, iclr = ld_s[...], cum_s[...], iclr_s[...]

        kk = k * prow(_K_K)
        kk = kk * lax.rsqrt(jnp.maximum(head_sum(kk * kk), 1e-24))
        k = k * (1.0 + (iclr - 1.0) * prow(_K_A))
        b_vec = kk * iclr

        e_neg = jnp.exp(-cum)
        a_t = -kk * jnp.exp(cum - log_decay)
        r_t = r * jnp.exp(cum)
        b_t = b_vec * e_neg
        k_t = k * e_neg
        bonus = head_sum(r * k * prow(_R_K)) * v

        units = [(j, p) for j in range(SLAB_CHUNKS) for p in range(n_pairs)]
        heads = range(2)
        blk = lambda x, u: x[u[0] * CHUNK:(u[0] + 1) * CHUNK, u[1] * PAIR:(u[1] + 1) * PAIR]
        cum_end = {u: cum[(u[0] + 1) * CHUNK - 1:(u[0] + 1) * CHUNK, u[1] * PAIR:(u[1] + 1) * PAIR]
                   for u in units}
        at = {u: [jnp.where(head_c[h], blk(a_t, u), 0.0) for h in heads] for u in units}
        rtb = {u: blk(r_t, u).astype(BF16) for u in units}
        vb = {u: blk(v, u).astype(BF16) for u in units}
        rt = {u: [jnp.where(head_c[h], rtb[u], zeros_b) for h in heads] for u in units}
        vh = {u: [jnp.where(head_c[h], vb[u], zeros_b) for h in heads] for u in units}
        xab = {u: [at[u][h].astype(BF16) for h in heads] for u in units}
        big = {u: jnp.where(keep4,
                            _dot_nt(jnp.concatenate([xab[u][0], rt[u][0], xab[u][1], rt[u][1]], axis=0),
                                    jnp.concatenate([blk(b_t, u), blk(k_t, u)], axis=0)), 0.0)
               for u in units}
        bot = {u: [big[u][(2 * h + 1) * CHUNK:(2 * h + 2) * CHUNK] for h in heads] for u in units}
        px = {u: [big[u][2 * h * CHUNK:(2 * h + 1) * CHUNK] for h in heads] for u in units}
        xa = at
        pxb = {u: [px[u][h].astype(BF16) for h in heads] for u in units}
        for _ in range(6):
            res = {u: [_dot(pxb[u][h][:, :CHUNK], jnp.concatenate([pxb[u][h], xab[u][h]], axis=1))
                       for h in heads] for u in units}
            px = {u: [res[u][h][:, :PAIR] + jnp.where(plane, 0.0, px[u][h]) for h in heads] for u in units}
            xa = {u: [res[u][h][:, PAIR:] + xa[u][h] for h in heads] for u in units}
            pxb = {u: [px[u][h].astype(BF16) for h in heads] for u in units}
            xab = {u: [xa[u][h].astype(BF16) for h in heads] for u in units}
        u_v = {u: [_dot(pxb[u][h], jnp.concatenate([zeros_b, vh[u][h]], axis=0)) for h in heads]
               for u in units}
        ry = {u: [_dot(bot[u][h],
                       jnp.concatenate([jnp.concatenate([xab[u][h], u_v[u][h].astype(BF16)], axis=1),
                                        jnp.concatenate([zeros_b, vh[u][h]], axis=1)], axis=0))
                  for h in heads] for u in units}
        mn = {}
        for u in units:
            e_end = jnp.exp(cum_end[u] - blk(cum, u))
            bk_h = jnp.concatenate([blk(b_vec, u) * e_end, blk(k, u) * e_end], axis=0)
            apuv = jnp.concatenate(
                [jnp.concatenate([xa[u][0] + xa[u][1], u_v[u][0] + u_v[u][1]], axis=1).astype(BF16),
                 jnp.concatenate([zeros_b, vb[u]], axis=1)], axis=0)
            mn[u] = _dot_tn(bk_h, apuv)
        states = list(states)
        gate = gate_s[...]
        next_low = None
        for u in units:
            j, p = u
            if j == 1 and next_low is None:
                next_low = lowrank_dots(jnp.minimum(i + 1, n_slabs - 1))
            m = mn[u][:, :PAIR] + jnp.where(eye_p, jnp.exp(cum_end[u]), 0.0)
            r_p = blk(r_t, u) + ry[u][0][:, :PAIR] + ry[u][1][:, :PAIR]
            out = _dot(jnp.concatenate([m, r_p], axis=0), states[p])
            states[p] = jnp.where(same_head, out[:PAIR] + mn[u][:, PAIR:], 0.0)
            y = out[PAIR:] + ry[u][0][:, PAIR:] + ry[u][1][:, PAIR:]
            yc = y - head_sum(y) * (1.0 / HEAD_DIM)
            var = head_sum(yc * yc) * (1.0 / HEAD_DIM)
            lanes = slice(p * PAIR, (p + 1) * PAIR)
            y = yc * lax.rsqrt(var + GN_EPS) * prow(_LNX_G)[:, lanes] + prow(_LNX_B)[:, lanes]
            o_ref[pl.ds(t0 + j * CHUNK, CHUNK), lanes] = ((y + blk(bonus, u)) * blk(gate, u)).astype(o_ref.dtype)
        hand_over(*next_low)
        return tuple(states)

    zero_state = jnp.zeros((PAIR, PAIR), F32)
    hand_over(*lowrank_dots(0))
    lax.fori_loop(0, n_slabs, slab_body, (zero_state,) * n_pairs)


def _rwkv(proj3, layer, pvec, lowrank_w):
    bsz, seq, _ = proj3.shape
    col = lambda blk: pl.BlockSpec((None, seq, RWKV_WIDTH), lambda b, blk=blk: (b, 0, blk))
    low = lambda blk: pl.BlockSpec((None, seq, LOW_WIDTH), lambda b, blk=blk: (b, 0, blk))
    full = lambda a: _layer_spec(a, layer)
    return pl.pallas_call(
        _rwkv_kernel,
        grid=(bsz,),
        in_specs=[col(0), col(1), col(2),
                  low(3 * RWKV_WIDTH // LOW_WIDTH), low(3 * RWKV_WIDTH // LOW_WIDTH + 1),
                  full(pvec), full(lowrank_w)],
        out_specs=pl.BlockSpec((None, seq, RWKV_WIDTH), lambda b: (b, 0, 0)),
        out_shape=jax.ShapeDtypeStruct((bsz, seq, RWKV_WIDTH), BF16),
        scratch_shapes=[pltpu.VMEM((SLAB_CHUNKS * CHUNK, RWKV_WIDTH), F32)] * 4,
        compiler_params=pltpu.CompilerParams(
            dimension_semantics=("parallel",), vmem_limit_bytes=VMEM_LIMIT),
        name="rwkv",
    )(proj3, proj3, proj3, proj3, proj3, pvec, lowrank_w)


def _outffn_kernel(x_ref, yr_ref, mixc_ref, wo_ref, wg_ref, wu_ref, wd_ref, g_ref, wn32_ref, o_ref, wn_ref):
    wn_ref[...] = wn32_ref[...].astype(BF16)
    half = x_ref.shape[0] // FFN_PARTS
    halves = [slice(n * half, (n + 1) * half) for n in range(FFN_PARTS)]
    mix = [jnp.dot(yr_ref[s, :], wo_ref[...], preferred_element_type=F32) + mixc_ref[s, :] for s in halves]
    x = [x_ref[s, :] + _rms(m, g_ref[1:2, :]) for s, m in zip(halves, mix)]
    h2 = [_rms(xs, g_ref[2:3, :]).astype(BF16) for xs in x]
    act = []
    for h in h2:
        gate = jnp.dot(h, wg_ref[...], preferred_element_type=F32)
        up = jnp.dot(h, wu_ref[...], preferred_element_type=F32)
        act.append((gate * jax.nn.sigmoid(gate) * up).astype(BF16))
    f = [jnp.dot(a, wd_ref[...], preferred_element_type=F32) for a in act]
    for s, xs, fs in zip(halves, x, f):
        o_ref[s, :] = xs + _rms(fs, g_ref[3:4, :])


def _outffn(x2, y_rwkv, mix_conv, layer, wo, wg, wu, wd, gains, w_in, next_layer):
    rows = x2.shape[0]
    n_tiles = rows // ROW_TILE
    in_rows = D_MODEL // n_tiles
    assert in_rows * n_tiles == D_MODEL and in_rows % BF16_ROWS == 0
    tile = lambda a: pl.BlockSpec((ROW_TILE, a.shape[1]), lambda i: (i, 0))
    whole = lambda a: pl.BlockSpec(a.shape, lambda i: (0, 0), pipeline_mode=pl.Buffered(1))
    return pl.pallas_call(
        _outffn_kernel,
        grid=(n_tiles,),
        in_specs=[tile(x2), tile(y_rwkv), tile(mix_conv),
                  _layer_spec(wo, layer, (RWKV_WIDTH, D_MODEL), pipeline_mode=pl.Buffered(1)),
                  whole(wg), whole(wu), whole(wd),
                  _layer_spec(gains, layer, pipeline_mode=pl.Buffered(1)),
                  pl.BlockSpec((None, in_rows, w_in.shape[2]), lambda i: (next_layer, i, 0))],
        out_specs=[tile(x2), pl.BlockSpec((in_rows, w_in.shape[2]), lambda i: (i, 0))],
        out_shape=[jax.ShapeDtypeStruct(x2.shape, F32),
                   jax.ShapeDtypeStruct(w_in.shape[1:], BF16)],
        compiler_params=pltpu.CompilerParams(
            dimension_semantics=("parallel",), vmem_limit_bytes=VMEM_LIMIT),
        name="outffn",
    )(x2, y_rwkv, mix_conv, wo, wg, wu, wd, gains, w_in)


def _fold_low_rank(mu, w):
    return (1.0 - mu)[..., None] * w, mu[..., None] * w


def kernel(x, w_in, mu_rkv, mu_wag, w0, w1, w2, a0, a1, a2, g1, g2, k_k, k_a, r_k, lnx_g, lnx_b,
           sc_conv_w, cf_conv_w, cf_conv_b, cf_ln_g, cf_ln_b, w_o, w_gate, w_up, w_down,
           pre_mix_g, post_mix_g, pre_ffn_g, post_ffn_g):
    bsz, seq, d = x.shape
    depth = w_in.shape[0]
    w1a, w1b = _fold_low_rank(mu_wag[:, 0], w1)
    a1a, a1b = _fold_low_rank(mu_wag[:, 1], a1)
    g1a, g1b = _fold_low_rank(mu_wag[:, 2], g1)
    w_low = jnp.concatenate([w1a, a1a, g1a, w1b, a1b, g1b], axis=2).astype(BF16)
    w_inb = w_in[0].astype(BF16)
    pvec = jnp.concatenate([
        mu_rkv.reshape(depth, 3, RWKV_WIDTH),
        jnp.stack([w0, a0, k_k, k_a, r_k.reshape(depth, RWKV_WIDTH), lnx_g, lnx_b], axis=1),
        jnp.zeros((depth, _PVEC_ROWS - 10, RWKV_WIDTH), F32)], axis=1)
    lowrank_w = jnp.concatenate([w2, a2, g2], axis=1).astype(BF16)
    cf_p = jnp.stack([cf_conv_b, cf_ln_g, cf_ln_b], axis=1)
    gains = jnp.stack([pre_mix_g, post_mix_g, pre_ffn_g, post_ffn_g], axis=1)
    wo = w_o.astype(BF16)

    x2 = x.reshape(bsz * seq, d)
    for l in range(depth):
        proj, mix_conv, wg, wu, wd = _inproj_conv(x2, l, gains, w_inb, w_low, wo, sc_conv_w, cf_conv_w, cf_p,
                                                  w_gate, w_up, w_down, seq)
        proj3 = proj.reshape(bsz, seq, RWKV_COLS)
        y_rwkv = _rwkv(proj3, l, pvec, lowrank_w).reshape(bsz * seq, RWKV_WIDTH)
        x2, w_inb = _outffn(x2, y_rwkv, mix_conv, l, wo, wg, wu, wd, gains, w_in, min(l + 1, depth - 1))
    return x2.reshape(bsz, seq, d)
```
